```python
import jax, jax.numpy as jnp
from jax import lax
import numpy as np


D_MODEL = 1024
BATCH = 4
SEQ = 8192
DEPTH = 4
DEC_BATCH = 16
DEC_SEQ = 4096
PAST_LEN = 128

GRID_W = 64
Q_BLOCK = 128
EPS = 1e-6
ROPE_THETA = 10000.0

MLA_HEADS = 8
MLA_NOPE = 128
MLA_ROPE = 64
MLA_QK = MLA_NOPE + MLA_ROPE
MLA_V = 128
Q_LORA = 384
KV_LORA = 256

GQA_HEADS = 8
GQA_KV_HEADS = 2
GQA_HD = 128

D_FF = 2816

IN_SPLITS = (Q_LORA, KV_LORA, MLA_ROPE, GQA_HEADS * GQA_HD, GQA_KV_HEADS * GQA_HD, GQA_KV_HEADS * GQA_HD, D_MODEL, D_MODEL)
IN_W = Q_LORA + KV_LORA + MLA_ROPE + GQA_HEADS * GQA_HD + 2 * GQA_KV_HEADS * GQA_HD + 2 * D_MODEL

kernel_name = 'hybrid_mla_axial_gqa_macaron_encoder'


def rmsnorm(x, g):
    xf = x.astype(jnp.float32)
    y = xf * lax.rsqrt(jnp.mean(xf * xf, axis=-1, keepdims=True) + EPS)
    return y.astype(x.dtype) * g


def axial_rope_tables(seq, rot_dim, dtype):
    rows = seq // GRID_W
    row = jnp.repeat(jnp.arange(rows, dtype=jnp.float32), GRID_W)
    col = jnp.tile(jnp.arange(GRID_W, dtype=jnp.float32), rows)
    n = rot_dim // 4
    freqs = ROPE_THETA ** (-jnp.arange(n, dtype=jnp.float32) / n)
    ang = jnp.concatenate([row[:, None] * freqs, col[:, None] * freqs], axis=-1)
    cos = jnp.cos(ang)[None, :, None, :].astype(dtype)
    sin = jnp.sin(ang)[None, :, None, :].astype(dtype)
    return cos, sin


def apply_rope(x, cos, sin):
    half = x.shape[-1] // 2
    x1, x2 = x[..., :half], x[..., half:]
    return jnp.concatenate([x1 * cos - x2 * sin, x2 * cos + x1 * sin], axis=-1)


def block_attention(q, k, v, scale):
    B, S, H, dq = q.shape
    Hk = k.shape[2]
    G = H // Hk
    dv = v.shape[-1]
    nb = S // Q_BLOCK
    qb = (q * scale).reshape(B, nb, Q_BLOCK, Hk, G, dq).transpose(1, 0, 2, 3, 4, 5)

    def one_block(q_blk):
        s = jnp.einsum('bqhgd,bkhd->bhgqk', q_blk, k).astype(jnp.float32)
        p = jax.nn.softmax(s, axis=-1).astype(v.dtype)
        return jnp.einsum('bhgqk,bkhe->bqhge', p, v)

    o = lax.map(one_block, qb)
    return o.transpose(1, 0, 2, 3, 4, 5).reshape(B, S, H * dv)


def swiglu(x, w_in, w_out):
    a, b = jnp.split(x @ w_in, 2, axis=-1)
    return (jax.nn.silu(a) * b) @ w_out


def split_columns(proj):
    parts = []
    start = 0
    for w in IN_SPLITS:
        parts.append(proj[..., start:start + w])
        start += w
    return parts


def token_mixing(h, w_in, g_cq, w_uq, g_ckv, w_ukv, g_qn, g_kn, w_o, rope_a, rope_b):
    B, S, _ = h.shape
    c_q, c_kv, k_r, q_b, k_b, v_b, gate_a, gate_b = split_columns(h @ w_in)

    q_a = (rmsnorm(c_q, g_cq) @ w_uq).reshape(B, S, MLA_HEADS, MLA_QK)
    q_a = jnp.concatenate([q_a[..., :MLA_NOPE], apply_rope(q_a[..., MLA_NOPE:], *rope_a)], axis=-1)
    kv_a = (rmsnorm(c_kv, g_ckv) @ w_ukv).reshape(B, S, MLA_HEADS, MLA_NOPE + MLA_V)
    k_r = apply_rope(k_r.reshape(B, S, 1, MLA_ROPE), *rope_a)
    k_a = jnp.concatenate([kv_a[..., :MLA_NOPE], jnp.broadcast_to(k_r, (B, S, MLA_HEADS, MLA_ROPE))], axis=-1)
    v_a = kv_a[..., MLA_NOPE:]
    o_a = block_attention(q_a, k_a, v_a, MLA_QK ** -0.5)

    q_b = apply_rope(rmsnorm(q_b.reshape(B, S, GQA_HEADS, GQA_HD), g_qn), *rope_b)
    k_b = apply_rope(rmsnorm(k_b.reshape(B, S, GQA_KV_HEADS, GQA_HD), g_kn), *rope_b)
    v_b = v_b.reshape(B, S, GQA_KV_HEADS, GQA_HD)
    o_b = block_attention(q_b, k_b, v_b, GQA_HD ** -0.5)

    merged = jax.nn.sigmoid(gate_a) * o_a + jax.nn.sigmoid(gate_b) * o_b
    return merged @ w_o


def trunk(x, norm_ffn1, w_ffn1_in, w_ffn1_out, norm_mix, w_in, g_cq, w_uq, g_ckv, w_ukv,
          g_qn, g_kn, w_o, norm_ffn2, w_ffn2_in, w_ffn2_out, norm_final):
    S = x.shape[1]
    rope_a = axial_rope_tables(S, MLA_ROPE, x.dtype)
    rope_b = axial_rope_tables(S, GQA_HD, x.dtype)
    for l in range(DEPTH):
        x = x + 0.5 * swiglu(rmsnorm(x, norm_ffn1[l]), w_ffn1_in[l], w_ffn1_out[l])
        h = rmsnorm(x, norm_mix[l])
        x = x + token_mixing(h, w_in[l], g_cq[l], w_uq[l], g_ckv[l], w_ukv[l],
                             g_qn[l], g_kn[l], w_o[l], rope_a, rope_b)
        x = x + 0.5 * swiglu(rmsnorm(x, norm_ffn2[l]), w_ffn2_in[l], w_ffn2_out[l])
    return rmsnorm(x, norm_final)


def setup_inputs(seed: int = 0) -> dict:
    key = jax.random.key(seed)
    ks = jax.random.split(key, 20)
    f32 = jnp.float32

    def w(k, shape):
        return jax.random.normal(k, shape, f32) * (shape[-2] ** -0.5)

    def gain(k, shape):
        return 1.0 + 0.05 * jax.random.normal(k, shape, f32)

    return {
        'x_prompt': jax.random.normal(ks[0], (BATCH, SEQ, D_MODEL), f32),
        'x_sample': jax.random.normal(ks[1], (DEC_BATCH, DEC_SEQ, D_MODEL), f32),
        'norm_ffn1': gain(ks[2], (DEPTH, D_MODEL)),
        'w_ffn1_in': w(ks[3], (DEPTH, D_MODEL, 2 * D_FF)),
        'w_ffn1_out': w(ks[4], (DEPTH, D_FF, D_MODEL)),
        'norm_mix': gain(ks[5], (DEPTH, D_MODEL)),
        'w_in': w(ks[6], (DEPTH, D_MODEL, IN_W)),
        'g_cq': gain(ks[7], (DEPTH, Q_LORA)),
        'w_uq': w(ks[8], (DEPTH, Q_LORA, MLA_HEADS * MLA_QK)),
        'g_ckv': gain(ks[9], (DEPTH, KV_LORA)),
        'w_ukv': w(ks[10], (DEPTH, KV_LORA, MLA_HEADS * (MLA_NOPE + MLA_V))),
        'g_qn': gain(ks[11], (DEPTH, GQA_HD)),
        'g_kn': gain(ks[12], (DEPTH, GQA_HD)),
        'w_o': w(ks[13], (DEPTH, D_MODEL, D_MODEL)),
        'norm_ffn2': gain(ks[14], (DEPTH, D_MODEL)),
        'w_ffn2_in': w(ks[15], (DEPTH, D_MODEL, 2 * D_FF)),
        'w_ffn2_out': w(ks[16], (DEPTH, D_FF, D_MODEL)),
        'norm_final': gain(ks[17], (D_MODEL,)),
    }


def reference(x_prompt, x_sample, norm_ffn1, w_ffn1_in, w_ffn1_out, norm_mix, w_in, g_cq, w_uq,
              g_ckv, w_ukv, g_qn, g_kn, w_o, norm_ffn2, w_ffn2_in, w_ffn2_out, norm_final):
    y_prompt = trunk(x_prompt, norm_ffn1, w_ffn1_in, w_ffn1_out, norm_mix, w_in, g_cq, w_uq,
                     g_ckv, w_ukv, g_qn, g_kn, w_o, norm_ffn2, w_ffn2_in, w_ffn2_out, norm_final)
    y_sample = trunk(x_sample, norm_ffn1, w_ffn1_in, w_ffn1_out, norm_mix, w_in, g_cq, w_uq,
                     g_ckv, w_ukv, g_qn, g_kn, w_o, norm_ffn2, w_ffn2_in, w_ffn2_out, norm_final)
    return (y_prompt, y_sample)
```

```python
import functools
import math

import jax
import jax.numpy as jnp
from jax import lax
from jax.experimental import pallas as pl
from jax.experimental.pallas import tpu as pltpu

F32 = jnp.float32
BF16 = jnp.bfloat16

EPS = 1e-6
ROPE_THETA = 10000.0
GRID_W = 64

MLA_HEADS = 8
MLA_NOPE = 128
MLA_ROPE = 64
MLA_QK = MLA_NOPE + MLA_ROPE
MLA_V = 128
Q_LORA = 384
KV_LORA = 256
GQA_HEADS = 8
GQA_KV_HEADS = 2
GQA_GROUP = GQA_HEADS // GQA_KV_HEADS
GQA_HD = 128
D_FF = 2816

LOG2E = math.log2(math.e)

VMEM_LIMIT_BYTES = 56 * 1024 * 1024
TOKEN_TILE = 512
FF_CHUNKS = ((0, 1536), (1536, 1280))

_NT = (((1,), (1,)), ((), ()))


def _dot(a, b):
    return jnp.dot(a, b, preferred_element_type=F32)


def _dot_nt(a, b):
    return lax.dot_general(a, b, _NT, preferred_element_type=F32)


def _rms_rows(x, g):
    return x * lax.rsqrt(jnp.mean(x * x, axis=-1, keepdims=True) + EPS) * g


def _rms_cols(x, g):
    return x * lax.rsqrt(jnp.mean(x * x, axis=0, keepdims=True) + EPS) * g


def _const_spec(shape):
    return pl.BlockSpec(shape, lambda *_: (0,) * len(shape), pipeline_mode=pl.Buffered(1))


def _ffn_kernel(*refs, has_wo, final_norm):
    refs = list(refs)
    x_ref = refs.pop(0)
    if has_wo:
        m_ref = refs.pop(0)
        wo_ref = refs.pop(0)
    g_ref, wa_ref, wb_ref, wout_ref = refs[:4]
    refs = refs[4:]
    if final_norm:
        gf_ref = refs.pop(0)
    (o_ref,) = refs

    x = x_ref[...]
    if has_wo:
        x = x + _dot(m_ref[...], wo_ref[...])
    xn = _rms_rows(x, g_ref[...]).astype(BF16)
    acc = None
    for start, size in FF_CHUNKS:
        a = _dot(xn, wa_ref[:, start:start + size])
        b = _dot(xn, wb_ref[:, start:start + size])
        h = (a * jax.nn.sigmoid(a) * b).astype(BF16)
        part = _dot(h, wout_ref[start:start + size, :])
        acc = part if acc is None else acc + part
    y = x + 0.5 * acc
    if final_norm:
        y = _rms_rows(y, gf_ref[...])
    o_ref[...] = y


def _ffn(x2d, gain, wa, wb, wout, merged=None, w_o=None, final_gain=None):
    n_tok, d = x2d.shape
    tm = min(TOKEN_TILE, n_tok)
    has_wo = merged is not None
    final_norm = final_gain is not None
    tile = pl.BlockSpec((tm, d), lambda i: (i, 0))
    args, specs = [x2d], [tile]
    if has_wo:
        args += [merged, w_o]
        specs += [tile, _const_spec(w_o.shape)]
    args += [gain, wa, wb, wout]
    specs += [_const_spec(gain.shape), _const_spec(wa.shape), _const_spec(wb.shape),
              _const_spec(wout.shape)]
    if final_norm:
        args.append(final_gain)
        specs.append(_const_spec(final_gain.shape))
    return pl.pallas_call(
        functools.partial(_ffn_kernel, has_wo=has_wo, final_norm=final_norm),
        grid=(n_tok // tm,),
        in_specs=specs,
        out_specs=tile,
        out_shape=jax.ShapeDtypeStruct((n_tok, d), F32),
        compiler_params=pltpu.CompilerParams(
            dimension_semantics=("arbitrary",), vmem_limit_bytes=VMEM_LIMIT_BYTES),
        name="ffn",
    )(*args)


def _proj_kernel(x_ref, gmix_ref, w1_ref, w2t_ref, gcq_ref, wuqt_ref, gckv_ref, wukvk_ref,
                 wukvvt_ref, gqn_ref, gkn_ref, ropek_a_ref, cosq_a_ref, sinq_a_ref,
                 cosk_b_ref, sink_b_ref, cosq_b_ref, sinq_b_ref,
                 qa_ref, ka_ref, va_ref, qb_ref, kb_ref, vb_ref, gate_ref):
    hb = _rms_rows(x_ref[...], gmix_ref[...]).astype(BF16)

    c = _dot(hb, w1_ref[...])
    o_ckv = Q_LORA
    o_kr = o_ckv + KV_LORA
    o_kb = o_kr + 2 * MLA_ROPE
    o_gate = o_kb + GQA_KV_HEADS * GQA_HD
    gate_ref[...] = jax.nn.sigmoid(c[:, o_gate:])

    cqn = _rms_rows(c[:, :Q_LORA], gcq_ref[...]).astype(BF16)
    ckvn = _rms_rows(c[:, o_ckv:o_kr], gckv_ref[...]).astype(BF16)

    kr2 = c[:, o_kr:o_kb] * ropek_a_ref[...]
    kr = (kr2 + pltpu.roll(kr2, MLA_ROPE, axis=1))[:, :MLA_ROPE].astype(BF16)
    k_nope = _dot(ckvn, wukvk_ref[...]).astype(BF16)
    for h in range(MLA_HEADS):
        ka_ref[h, :, :MLA_NOPE] = k_nope[:, h * MLA_NOPE:(h + 1) * MLA_NOPE]
        ka_ref[h, :, MLA_NOPE:] = kr

    for h in range(GQA_KV_HEADS):
        kb = _rms_rows(c[:, o_kb + h * GQA_HD:o_kb + (h + 1) * GQA_HD], gkn_ref[...])
        kb = kb * cosk_b_ref[...] + pltpu.roll(kb, GQA_HD // 2, axis=1) * sink_b_ref[...]
        kb_ref[h] = kb.astype(BF16)

    va = _dot_nt(wukvvt_ref[...], ckvn)
    for h in range(MLA_HEADS):
        va_ref[h] = va[h * MLA_V:(h + 1) * MLA_V].astype(BF16)

    qa = _dot_nt(wuqt_ref[...], cqn)
    sa = MLA_QK ** -0.5 * LOG2E
    half = MLA_ROPE // 2
    cos_a, sin_a = cosq_a_ref[...], sinq_a_ref[...]
    for h in range(MLA_HEADS):
        base = h * MLA_QK
        x1 = qa[base + MLA_NOPE:base + MLA_NOPE + half]
        x2 = qa[base + MLA_NOPE + half:base + MLA_QK]
        qa_ref[h, :MLA_NOPE] = (qa[base:base + MLA_NOPE] * sa).astype(BF16)
        qa_ref[h, MLA_NOPE:MLA_NOPE + half] = ((x1 * cos_a - x2 * sin_a) * sa).astype(BF16)
        qa_ref[h, MLA_NOPE + half:] = ((x2 * cos_a + x1 * sin_a) * sa).astype(BF16)

    qv = _dot_nt(w2t_ref[...], hb)
    sb = GQA_HD ** -0.5 * LOG2E
    half = GQA_HD // 2
    cos_b, sin_b = cosq_b_ref[...], sinq_b_ref[...]
    for h in range(GQA_HEADS):
        q = _rms_cols(qv[h * GQA_HD:(h + 1) * GQA_HD], gqn_ref[...])
        x1, x2 = q[:half], q[half:]
        qb_ref[h, :half] = ((x1 * cos_b - x2 * sin_b) * sb).astype(BF16)
        qb_ref[h, half:] = ((x2 * cos_b + x1 * sin_b) * sb).astype(BF16)
    o_vb = GQA_HEADS * GQA_HD
    for h in range(GQA_KV_HEADS):
        vb_ref[h] = qv[o_vb + h * GQA_HD:o_vb + (h + 1) * GQA_HD].astype(BF16)


def _proj(x, weights, tables):
    B, S, D = x.shape
    tm = min(TOKEN_TILE, S)
    nt = S // tm
    (gmix, w1, w2t, gcq, wuqt, gckv, wukvk, wukvvt, gqn, gkn) = weights
    ropek_a, cosq_a, sinq_a, cosk_b, sink_b, cosq_b, sinq_b = tables

    def tok_table(t):
        return pl.BlockSpec((tm, t.shape[1]), lambda b, i: (i, 0))

    def feat_table(t):
        return pl.BlockSpec((t.shape[0], tm), lambda b, i: (0, i))

    def feat_out(heads, width):
        return (jax.ShapeDtypeStruct((B, heads, nt, width, tm), BF16),
                pl.BlockSpec((None, heads, None, width, tm), lambda b, i: (b, 0, i, 0, 0)))

    def tok_out(heads, width):
        return (jax.ShapeDtypeStruct((B, heads, S, width), BF16),
                pl.BlockSpec((None, heads, tm, width), lambda b, i: (b, 0, i, 0)))

    outs = [feat_out(MLA_HEADS, MLA_QK), tok_out(MLA_HEADS, MLA_QK), feat_out(MLA_HEADS, MLA_V),
            feat_out(GQA_HEADS, GQA_HD), tok_out(GQA_KV_HEADS, GQA_HD),
            feat_out(GQA_KV_HEADS, GQA_HD),
            (jax.ShapeDtypeStruct((B, S, 2 * D), F32),
             pl.BlockSpec((None, tm, 2 * D), lambda b, i: (b, i, 0)))]
    in_specs = [pl.BlockSpec((None, tm, D), lambda b, i: (b, i, 0))]
    in_specs += [_const_spec(w.shape) for w in weights]
    in_specs += [tok_table(ropek_a), feat_table(cosq_a), feat_table(sinq_a), tok_table(cosk_b),
                 tok_table(sink_b), feat_table(cosq_b), feat_table(sinq_b)]
    return pl.pallas_call(
        _proj_kernel,
        grid=(B, nt),
        in_specs=in_specs,
        out_specs=[o[1] for o in outs],
        out_shape=[o[0] for o in outs],
        compiler_params=pltpu.CompilerParams(
            dimension_semantics=("arbitrary", "arbitrary"), vmem_limit_bytes=VMEM_LIMIT_BYTES),
        name="proj",
    )(x, *weights, *tables)


def _attend(q_ref, k_ref, v_ref, acc_ref):
    q = q_ref[...]
    n_chunks, _, tk = v_ref.shape
    tq = q.shape[1]
    acc_ref[...] = jnp.zeros_like(acc_ref)

    def body(c, carry):
        m, l = carry
        k = k_ref[pl.ds(pl.multiple_of(c * tk, tk), tk), :]
        s = _dot(k, q)
        m_new = jnp.maximum(m, jnp.max(s, axis=0, keepdims=True))
        alpha = jnp.exp2(m - m_new)
        p = jnp.exp2(s - m_new)
        l = alpha * l + jnp.sum(p, axis=0, keepdims=True)
        acc_ref[...] = alpha * acc_ref[...] + _dot(v_ref[c], p.astype(BF16))
        return m_new, l

    m0 = jnp.full((1, tq), -jnp.inf, F32)
    l0 = jnp.zeros((1, tq), F32)
    _, l = lax.fori_loop(0, n_chunks, body, (m0, l0))
    return acc_ref[...] / l


def _attn_kernel(qa_ref, ka_ref, va_ref, qb_ref, kb_ref, vb_ref, ga_ref, gb_ref, o_ref, acc_ref):
    o_a = _attend(qa_ref, ka_ref, va_ref, acc_ref)
    o_b = _attend(qb_ref, kb_ref, vb_ref, acc_ref)
    o_ref[...] = (ga_ref[...] * o_a.T + gb_ref[...] * o_b.T).astype(BF16)


def _attn(qa, ka, va, qb, kb, vb, gates):
    B, H, nt, _, tm = qa.shape
    S = ka.shape[2]

    def q_spec(width):
        return pl.BlockSpec((None, None, None, width, tm), lambda b, h, t: (b, h, t, 0, 0))

    in_specs = [
        q_spec(MLA_QK),
        pl.BlockSpec((None, None, S, MLA_QK), lambda b, h, t: (b, h, 0, 0)),
        pl.BlockSpec((None, None, nt, MLA_V, tm), lambda b, h, t: (b, h, 0, 0, 0)),
        q_spec(GQA_HD),
        pl.BlockSpec((None, None, S, GQA_HD), lambda b, h, t: (b, h // GQA_GROUP, 0, 0)),
        pl.BlockSpec((None, None, nt, GQA_HD, tm), lambda b, h, t: (b, h // GQA_GROUP, 0, 0, 0)),
        pl.BlockSpec((None, tm, MLA_V), lambda b, h, t: (b, t, h)),
        pl.BlockSpec((None, tm, GQA_HD), lambda b, h, t: (b, t, H + h)),
    ]
    return pl.pallas_call(
        _attn_kernel,
        grid=(B, H, nt),
        in_specs=in_specs,
        out_specs=pl.BlockSpec((None, tm, MLA_V), lambda b, h, t: (b, t, h)),
        out_shape=jax.ShapeDtypeStruct((B, S, H * MLA_V), BF16),
        scratch_shapes=[pltpu.VMEM((MLA_V, tm), F32)],
        compiler_params=pltpu.CompilerParams(
            dimension_semantics=("arbitrary", "arbitrary", "arbitrary"),
            vmem_limit_bytes=VMEM_LIMIT_BYTES),
        name="attn",
    )(qa, ka, va, qb, kb, vb, gates, gates)


def _rope_tables(S):
    rows = S // GRID_W
    row = jnp.repeat(jnp.arange(rows, dtype=F32), GRID_W)
    col = jnp.tile(jnp.arange(GRID_W, dtype=F32), rows)

    def cos_sin(rot_dim):
        n = rot_dim // 4
        freqs = ROPE_THETA ** (-jnp.arange(n, dtype=F32) / n)
        ang = jnp.concatenate([row[:, None] * freqs, col[:, None] * freqs], axis=-1)
        return jnp.cos(ang), jnp.sin(ang)

    cos_a, sin_a = cos_sin(MLA_ROPE)
    cos_b, sin_b = cos_sin(GQA_HD)
    ropek_a = jnp.concatenate([cos_a, cos_a, sin_a, sin_a], axis=-1)
    cosk_b = jnp.concatenate([cos_b, cos_b], axis=-1)
    sink_b = jnp.concatenate([-sin_b, sin_b], axis=-1)
    return (ropek_a, cos_a.T, sin_a.T, cosk_b, sink_b, cos_b.T, sin_b.T)


def _prep_layer(l, norm_ffn1, w_ffn1_in, w_ffn1_out, norm_mix, w_in, g_cq, w_uq, g_ckv, w_ukv,
                g_qn, g_kn, w_o, norm_ffn2, w_ffn2_in, w_ffn2_out):
    D = w_in.shape[1]

    def ffn_w(norm, wi, wo_):
        wa, wb = jnp.split(wi[l].astype(BF16), 2, axis=-1)
        return norm[l][None, :], wa, wb, wo_[l].astype(BF16)

    wi = w_in[l].astype(BF16)
    edges = [0]
    for w in (Q_LORA, KV_LORA, MLA_ROPE, GQA_HEADS * GQA_HD, GQA_KV_HEADS * GQA_HD,
              GQA_KV_HEADS * GQA_HD, D, D):
        edges.append(edges[-1] + w)
    c_q, c_kv, k_r, q_b, k_b, v_b, gate_a, gate_b = [wi[:, edges[i]:edges[i + 1]] for i in range(8)]
    half = MLA_ROPE // 2
    k_r_rot = jnp.concatenate([-k_r[:, half:], k_r[:, :half]], axis=-1)
    w1 = jnp.concatenate([c_q, c_kv, k_r, k_r_rot, k_b, gate_a, gate_b], axis=-1)
    w2t = jnp.concatenate([q_b, v_b], axis=-1).T
    wukv = w_ukv[l].astype(BF16).reshape(KV_LORA, MLA_HEADS, MLA_NOPE + MLA_V)
    wukvk = wukv[:, :, :MLA_NOPE].reshape(KV_LORA, MLA_HEADS * MLA_NOPE)
    wukvvt = wukv[:, :, MLA_NOPE:].reshape(KV_LORA, MLA_HEADS * MLA_V).T
    proj_w = (norm_mix[l][None, :], w1, w2t, g_cq[l][None, :], w_uq[l].astype(BF16).T,
              g_ckv[l][None, :], wukvk, wukvvt, g_qn[l][:, None], g_kn[l][None, :])
    return (ffn_w(norm_ffn1, w_ffn1_in, w_ffn1_out), proj_w, w_o[l].astype(BF16),
            ffn_w(norm_ffn2, w_ffn2_in, w_ffn2_out))


def _trunk(x, layers, norm_final):
    B, S, D = x.shape
    tables = _rope_tables(S)
    x2d = x.reshape(B * S, D)
    for i, (ffn1_w, proj_w, w_o, ffn2_w) in enumerate(layers):
        x2d = _ffn(x2d, *ffn1_w)
        qa, ka, va, qb, kb, vb, gates = _proj(x2d.reshape(B, S, D), proj_w, tables)
        merged = _attn(qa, ka, va, qb, kb, vb, gates)
        final_gain = norm_final[None, :] if i == len(layers) - 1 else None
        x2d = _ffn(x2d, *ffn2_w, merged=merged.reshape(B * S, D), w_o=w_o, final_gain=final_gain)
    return x2d.reshape(B, S, D)


def kernel(x_prompt, x_sample, norm_ffn1, w_ffn1_in, w_ffn1_out, norm_mix, w_in, g_cq, w_uq, g_ckv, w_ukv, g_qn, g_kn, w_o, norm_ffn2, w_ffn2_in, w_ffn2_out, norm_final):
    depth = w_in.shape[0]
    layers = [_prep_layer(l, norm_ffn1, w_ffn1_in, w_ffn1_out, norm_mix, w_in, g_cq, w_uq, g_ckv,
                          w_ukv, g_qn, g_kn, w_o, norm_ffn2, w_ffn2_in, w_ffn2_out)
              for l in range(depth)]
    return (_trunk(x_prompt, layers, norm_final), _trunk(x_sample, layers, norm_final))
```

```python
import functools
import math

import jax
import jax.numpy as jnp
from jax import lax
from jax.experimental import pallas as pl
from jax.experimental.pallas import tpu as pltpu

F32 = jnp.float32
BF16 = jnp.bfloat16

EPS = 1e-6
ROPE_THETA = 10000.0
GRID_W = 64

MLA_HEADS = 8
MLA_NOPE = 128
MLA_ROPE = 64
MLA_QK = MLA_NOPE + MLA_ROPE
MLA_V = 128
Q_LORA = 384
KV_LORA = 256
GQA_HEADS = 8
GQA_KV_HEADS = 2
GQA_GROUP = GQA_HEADS // GQA_KV_HEADS
GQA_HD = 128
D_FF = 2816

LOG2E = math.log2(math.e)

VMEM_LIMIT_BYTES = 56 * 1024 * 1024
TOKEN_TILE = 512
FF_CHUNKS = ((0, 1536), (1536, 1280))

_NT = (((1,), (1,)), ((), ()))


def _dot(a, b):
    return jnp.dot(a, b, preferred_element_type=F32)


def _dot_nt(a, b):
    return lax.dot_general(a, b, _NT, preferred_element_type=F32)


def _rms_rows(x, g):
    return x * lax.rsqrt(jnp.mean(x * x, axis=-1, keepdims=True) + EPS) * g


def _rms_cols(x, g):
    return x * lax.rsqrt(jnp.mean(x * x, axis=0, keepdims=True) + EPS) * g


def _const_spec(shape):
    return pl.BlockSpec(shape, lambda *_: (0,) * len(shape), pipeline_mode=pl.Buffered(1))


def _ffn_kernel(*refs, has_wo, final_norm):
    refs = list(refs)
    x_ref = refs.pop(0)
    if has_wo:
        m_ref = refs.pop(0)
        wo_ref = refs.pop(0)
    g_ref, wa_ref, wb_ref, wout_ref = refs[:4]
    refs = refs[4:]
    if final_norm:
        gf_ref = refs.pop(0)
    (o_ref,) = refs

    x = x_ref[...]
    if has_wo:
        x = x + _dot(m_ref[...], wo_ref[...])
    xn = _rms_rows(x, g_ref[...]).astype(BF16)
    acc = None
    for start, size in FF_CHUNKS:
        a = _dot(xn, wa_ref[:, start:start + size])
        b = _dot(xn, wb_ref[:, start:start + size])
        h = (a * jax.nn.sigmoid(a) * b).astype(BF16)
        part = _dot(h, wout_ref[start:start + size, :])
        acc = part if acc is None else acc + part
    y = x + 0.5 * acc
    if final_norm:
        y = _rms_rows(y, gf_ref[...])
    o_ref[...] = y


def _ffn(x2d, gain, wa, wb, wout, merged=None, w_o=None, final_gain=None):
    n_tok, d = x2d.shape
    tm = min(TOKEN_TILE, n_tok)
    has_wo = merged is not None
    final_norm = final_gain is not None
    tile = pl.BlockSpec((tm, d), lambda i: (i, 0))
    args, specs = [x2d], [tile]
    if has_wo:
        args += [merged, w_o]
        specs += [tile, _const_spec(w_o.shape)]
    args += [gain, wa, wb, wout]
    specs += [_const_spec(gain.shape), _const_spec(wa.shape), _const_spec(wb.shape),
              _const_spec(wout.shape)]
    if final_norm:
        args.append(final_gain)
        specs.append(_const_spec(final_gain.shape))
    return pl.pallas_call(
        functools.partial(_ffn_kernel, has_wo=has_wo, final_norm=final_norm),
        grid=(n_tok // tm,),
        in_specs=specs,
        out_specs=tile,
        out_shape=jax.ShapeDtypeStruct((n_tok, d), F32),
        compiler_params=pltpu.CompilerParams(
            dimension_semantics=("arbitrary",), vmem_limit_bytes=VMEM_LIMIT_BYTES),
        name="ffn",
    )(*args)


def _proj_kernel(x_ref, gmix_ref, w1_ref, w2t_ref, gcq_ref, wuqt_ref, gckv_ref, wukvk_ref,
                 wukvvt_ref, gqn_ref, gkn_ref, ropek_a_ref, cosq_a_ref, sinq_a_ref,
                 cosk_b_ref, sink_b_ref, cosq_b_ref, sinq_b_ref,
                 qa_ref, ka_ref, va_ref, qb_ref, kb_ref, vb_ref, gate_ref):
    hb = _rms_rows(x_ref[...], gmix_ref[...]).astype(BF16)

    c = _dot(hb, w1_ref[...])
    o_ckv = Q_LORA
    o_kr = o_ckv + KV_LORA
    o_kb = o_kr + 2 * MLA_ROPE
    o_gate = o_kb + GQA_KV_HEADS * GQA_HD
    gate_ref[...] = jax.nn.sigmoid(c[:, o_gate:])

    cqn = _rms_rows(c[:, :Q_LORA], gcq_ref[...]).astype(BF16)
    ckvn = _rms_rows(c[:, o_ckv:o_kr], gckv_ref[...]).astype(BF16)

    kr2 = c[:, o_kr:o_kb] * ropek_a_ref[...]
    kr = (kr2 + pltpu.roll(kr2, MLA_ROPE, axis=1))[:, :MLA_ROPE].astype(BF16)
    k_nope = _dot(ckvn, wukvk_ref[...]).astype(BF16)
    for h in range(MLA_HEADS):
        ka_ref[h, :, :MLA_NOPE] = k_nope[:, h * MLA_NOPE:(h + 1) * MLA_NOPE]
        ka_ref[h, :, MLA_NOPE:] = kr

    for h in range(GQA_KV_HEADS):
        kb = _rms_rows(c[:, o_kb + h * GQA_HD:o_kb + (h + 1) * GQA_HD], gkn_ref[...])
        kb = kb * cosk_b_ref[...] + pltpu.roll(kb, GQA_HD // 2, axis=1) * sink_b_ref[...]
        kb_ref[h] = kb.astype(BF16)

    va = _dot_nt(wukvvt_ref[...], ckvn)
    for h in range(MLA_HEADS):
        va_ref[h] = va[h * MLA_V:(h + 1) * MLA_V].astype(BF16)

    qa = _dot_nt(wuqt_ref[...], cqn)
    sa = MLA_QK ** -0.5 * LOG2E
    half = MLA_ROPE // 2
    cos_a, sin_a = cosq_a_ref[...], sinq_a_ref[...]
    for h in range(MLA_HEADS):
        base = h * MLA_QK
        x1 = qa[base + MLA_NOPE:base + MLA_NOPE + half]
        x2 = qa[base + MLA_NOPE + half:base + MLA_QK]
        qa_ref[h, :MLA_NOPE] = (qa[base:base + MLA_NOPE] * sa).astype(BF16)
        qa_ref[h, MLA_NOPE:MLA_NOPE + half] = ((x1 * cos_a - x2 * sin_a) * sa).astype(BF16)
        qa_ref[h, MLA_NOPE + half:] = ((x2 * cos_a + x1 * sin_a) * sa).astype(BF16)

    qv = _dot_nt(w2t_ref[...], hb)
    sb = GQA_HD ** -0.5 * LOG2E
    half = GQA_HD // 2
    cos_b, sin_b = cosq_b_ref[...], sinq_b_ref[...]
    for h in range(GQA_HEADS):
        q = _rms_cols(qv[h * GQA_HD:(h + 1) * GQA_HD], gqn_ref[...])
        x1, x2 = q[:half], q[half:]
        qb_ref[h, :half] = ((x1 * cos_b - x2 * sin_b) * sb).astype(BF16)
        qb_ref[h, half:] = ((x2 * cos_b + x1 * sin_b) * sb).astype(BF16)
    o_vb = GQA_HEADS * GQA_HD
    for h in range(GQA_KV_HEADS):
        vb_ref[h] = qv[o_vb + h * GQA_HD:o_vb + (h + 1) * GQA_HD].astype(BF16)


def _proj(x, weights, tables):
    B, S, D = x.shape
    tm = min(TOKEN_TILE, S)
    nt = S // tm
    (gmix, w1, w2t, gcq, wuqt, gckv, wukvk, wukvvt, gqn, gkn) = weights
    ropek_a, cosq_a, sinq_a, cosk_b, sink_b, cosq_b, sinq_b = tables

    def tok_table(t):
        return pl.BlockSpec((tm, t.shape[1]), lambda b, i: (i, 0))

    def feat_table(t):
        return pl.BlockSpec((t.shape[0], tm), lambda b, i: (0, i))

    def feat_out(heads, width):
        return (jax.ShapeDtypeStruct((B, heads, nt, width, tm), BF16),
                pl.BlockSpec((None, heads, None, width, tm), lambda b, i: (b, 0, i, 0, 0)))

    def tok_out(heads, width):
        return (jax.ShapeDtypeStruct((B, heads, S, width), BF16),
                pl.BlockSpec((None, heads, tm, width), lambda b, i: (b, 0, i, 0)))

    outs = [feat_out(MLA_HEADS, MLA_QK), tok_out(MLA_HEADS, MLA_QK), feat_out(MLA_HEADS, MLA_V),
            feat_out(GQA_HEADS, GQA_HD), tok_out(GQA_KV_HEADS, GQA_HD),
            feat_out(GQA_KV_HEADS, GQA_HD),
            (jax.ShapeDtypeStruct((B, S, 2 * D), F32),
             pl.BlockSpec((None, tm, 2 * D), lambda b, i: (b, i, 0)))]
    in_specs = [pl.BlockSpec((None, tm, D), lambda b, i: (b, i, 0))]
    in_specs += [_const_spec(w.shape) for w in weights]
    in_specs += [tok_table(ropek_a), feat_table(cosq_a), feat_table(sinq_a), tok_table(cosk_b),
                 tok_table(sink_b), feat_table(cosq_b), feat_table(sinq_b)]
    return pl.pallas_call(
        _proj_kernel,
        grid=(B, nt),
        in_specs=in_specs,
        out_specs=[o[1] for o in outs],
        out_shape=[o[0] for o in outs],
        compiler_params=pltpu.CompilerParams(
            dimension_semantics=("arbitrary", "arbitrary"), vmem_limit_bytes=VMEM_LIMIT_BYTES),
        name="proj",
    )(x, *weights, *tables)


def _attn_kernel(qa_ref, ka_ref, va_ref, qb_ref, kb_ref, vb_ref, ga_ref, gb_ref, o_ref,
                 s_ref, p_ref, acc_ref):
    q_refs, k_refs, v_refs = (qa_ref, qb_ref), (ka_ref, kb_ref), (va_ref, vb_ref)
    n_chunks, _, tk = va_ref.shape
    tq = qa_ref.shape[1]
    assert n_chunks == 1 or n_chunks % 2 == 0
    q = [r[...] for r in q_refs]

    def scores(c, buf):
        for i in range(2):
            k = k_refs[i][pl.ds(pl.multiple_of(c * tk, tk), tk), :]
            s_ref[i, buf] = _dot(k, q[i])

    def softmax(buf, state):
        new_state = []
        for i in range(2):
            m, l, _ = state[i]
            s = s_ref[i, buf]
            m_new = jnp.maximum(m, jnp.max(s, axis=0, keepdims=True))
            alpha = jnp.exp2(m - m_new)
            p = jnp.exp2(s - m_new)
            p_ref[i, buf] = p.astype(BF16)
            new_state.append((m_new, alpha * l + jnp.sum(p, axis=0, keepdims=True), alpha))
        return new_state

    def values(c, buf, state):
        for i in range(2):
            alpha = state[i][2]
            acc_ref[i] = alpha * acc_ref[i] + _dot(v_refs[i][c], p_ref[i, buf])

    def step(c, buf, state, first=False, last=False):
        if not last:
            scores(c + 1, 1 - buf)
        new_state = softmax(buf, state)
        if not first:
            values(c - 1, 1 - buf, state)
        return new_state

    acc_ref[...] = jnp.zeros_like(acc_ref)
    neg_inf = jnp.full((1, tq), -jnp.inf, F32)
    zero = jnp.zeros((1, tq), F32)
    state = [(neg_inf, zero, zero)] * 2
    scores(0, 0)
    state = step(0, 0, state, first=True, last=n_chunks == 1)
    if n_chunks > 1:
        def pair(i, flat):
            st = [tuple(flat[:3]), tuple(flat[3:])]
            st = step(2 * i + 1, 1, st)
            st = step(2 * i + 2, 0, st)
            return tuple(st[0]) + tuple(st[1])

        flat = lax.fori_loop(0, (n_chunks - 2) // 2, pair, tuple(state[0]) + tuple(state[1]))
        state = [tuple(flat[:3]), tuple(flat[3:])]
        state = step(n_chunks - 1, 1, state, last=True)
    values(n_chunks - 1, (n_chunks - 1) % 2, state)

    o_a = acc_ref[0] / state[0][1]
    o_b = acc_ref[1] / state[1][1]
    o_ref[...] = (ga_ref[...] * o_a.T + gb_ref[...] * o_b.T).astype(BF16)


def _attn(qa, ka, va, qb, kb, vb, gates):
    B, H, nt, _, tm = qa.shape
    S = ka.shape[2]

    def q_spec(width):
        return pl.BlockSpec((None, None, None, width, tm), lambda b, h, t: (b, h, t, 0, 0))

    in_specs = [
        q_spec(MLA_QK),
        pl.BlockSpec((None, None, S, MLA_QK), lambda b, h, t: (b, h, 0, 0)),
        pl.BlockSpec((None, None, nt, MLA_V, tm), lambda b, h, t: (b, h, 0, 0, 0)),
        q_spec(GQA_HD),
        pl.BlockSpec((None, None, S, GQA_HD), lambda b, h, t: (b, h // GQA_GROUP, 0, 0)),
        pl.BlockSpec((None, None, nt, GQA_HD, tm), lambda b, h, t: (b, h // GQA_GROUP, 0, 0, 0)),
        pl.BlockSpec((None, tm, MLA_V), lambda b, h, t: (b, t, h)),
        pl.BlockSpec((None, tm, GQA_HD), lambda b, h, t: (b, t, H + h)),
    ]
    return pl.pallas_call(
        _attn_kernel,
        grid=(B, H, nt),
        in_specs=in_specs,
        out_specs=pl.BlockSpec((None, tm, MLA_V), lambda b, h, t: (b, t, h)),
        out_shape=jax.ShapeDtypeStruct((B, S, H * MLA_V), BF16),
        scratch_shapes=[pltpu.VMEM((2, 2, tm, tm), F32),
                        pltpu.VMEM((2, 2, tm, tm), BF16),
                        pltpu.VMEM((2, MLA_V, tm), F32)],
        compiler_params=pltpu.CompilerParams(
            dimension_semantics=("arbitrary", "arbitrary", "arbitrary"),
            vmem_limit_bytes=VMEM_LIMIT_BYTES),
        name="attn",
    )(qa, ka, va, qb, kb, vb, gates, gates)


def _rope_tables(S):
    rows = S // GRID_W
    row = jnp.repeat(jnp.arange(rows, dtype=F32), GRID_W)
    col = jnp.tile(jnp.arange(GRID_W, dtype=F32), rows)

    def cos_sin(rot_dim):
        n = rot_dim // 4
        freqs = ROPE_THETA ** (-jnp.arange(n, dtype=F32) / n)
        ang = jnp.concatenate([row[:, None] * freqs, col[:, None] * freqs], axis=-1)
        return jnp.cos(ang), jnp.sin(ang)

    cos_a, sin_a = cos_sin(MLA_ROPE)
    cos_b, sin_b = cos_sin(GQA_HD)
    ropek_a = jnp.concatenate([cos_a, cos_a, sin_a, sin_a], axis=-1)
    cosk_b = jnp.concatenate([cos_b, cos_b], axis=-1)
    sink_b = jnp.concatenate([-sin_b, sin_b], axis=-1)
    return (ropek_a, cos_a.T, sin_a.T, cosk_b, sink_b, cos_b.T, sin_b.T)


def _prep_layer(l, norm_ffn1, w_ffn1_in, w_ffn1_out, norm_mix, w_in, g_cq, w_uq, g_ckv, w_ukv,
                g_qn, g_kn, w_o, norm_ffn2, w_ffn2_in, w_ffn2_out):
    D = w_in.shape[1]

    def ffn_w(norm, wi, wo_):
        wa, wb = jnp.split(wi[l].astype(BF16), 2, axis=-1)
        return norm[l][None, :], wa, wb, wo_[l].astype(BF16)

    wi = w_in[l].astype(BF16)
    edges = [0]
    for w in (Q_LORA, KV_LORA, MLA_ROPE, GQA_HEADS * GQA_HD, GQA_KV_HEADS * GQA_HD,
              GQA_KV_HEADS * GQA_HD, D, D):
        edges.append(edges[-1] + w)
    c_q, c_kv, k_r, q_b, k_b, v_b, gate_a, gate_b = [wi[:, edges[i]:edges[i + 1]] for i in range(8)]
    half = MLA_ROPE // 2
    k_r_rot = jnp.concatenate([-k_r[:, half:], k_r[:, :half]], axis=-1)
    w1 = jnp.concatenate([c_q, c_kv, k_r, k_r_rot, k_b, gate_a, gate_b], axis=-1)
    w2t = jnp.concatenate([q_b, v_b], axis=-1).T
    wukv = w_ukv[l].astype(BF16).reshape(KV_LORA, MLA_HEADS, MLA_NOPE + MLA_V)
    wukvk = wukv[:, :, :MLA_NOPE].reshape(KV_LORA, MLA_HEADS * MLA_NOPE)
    wukvvt = wukv[:, :, MLA_NOPE:].reshape(KV_LORA, MLA_HEADS * MLA_V).T
    proj_w = (norm_mix[l][None, :], w1, w2t, g_cq[l][None, :], w_uq[l].astype(BF16).T,
              g_ckv[l][None, :], wukvk, wukvvt, g_qn[l][:, None], g_kn[l][None, :])
    return (ffn_w(norm_ffn1, w_ffn1_in, w_ffn1_out), proj_w, w_o[l].astype(BF16),
            ffn_w(norm_ffn2, w_ffn2_in, w_ffn2_out))


def _trunk(x, layers, norm_final):
    B, S, D = x.shape
    tables = _rope_tables(S)
    x2d = x.reshape(B * S, D)
    for i, (ffn1_w, proj_w, w_o, ffn2_w) in enumerate(layers):
        x2d = _ffn(x2d, *ffn1_w)
        qa, ka, va, qb, kb, vb, gates = _proj(x2d.reshape(B, S, D), proj_w, tables)
        merged = _attn(qa, ka, va, qb, kb, vb, gates)
        final_gain = norm_final[None, :] if i == len(layers) - 1 else None
        x2d = _ffn(x2d, *ffn2_w, merged=merged.reshape(B * S, D), w_o=w_o, final_gain=final_gain)
    return x2d.reshape(B, S, D)


def kernel(x_prompt, x_sample, norm_ffn1, w_ffn1_in, w_ffn1_out, norm_mix, w_in, g_cq, w_uq, g_ckv, w_ukv, g_qn, g_kn, w_o, norm_ffn2, w_ffn2_in, w_ffn2_out, norm_final):
    depth = w_in.shape[0]
    layers = [_prep_layer(l, norm_ffn1, w_ffn1_in, w_ffn1_out, norm_mix, w_in, g_cq, w_uq, g_ckv,
                          w_ukv, g_qn, g_kn, w_o, norm_ffn2, w_ffn2_in, w_ffn2_out)
              for l in range(depth)]
    return (_trunk(x_prompt, layers, norm_final), _trunk(x_sample, layers, norm_final))
```

```python
import functools
import math

import jax
import jax.numpy as jnp
from jax import lax
from jax.experimental import pallas as pl
from jax.experimental.pallas import tpu as pltpu

F32 = jnp.float32
BF16 = jnp.bfloat16

EPS = 1e-6
ROPE_THETA = 10000.0
GRID_W = 64

MLA_HEADS = 8
MLA_NOPE = 128
MLA_ROPE = 64
MLA_QK = MLA_NOPE + MLA_ROPE
MLA_V = 128
Q_LORA = 384
KV_LORA = 256
GQA_HEADS = 8
GQA_KV_HEADS = 2
GQA_GROUP = GQA_HEADS // GQA_KV_HEADS
GQA_HD = 128
D_FF = 2816

LOG2E = math.log2(math.e)

VMEM_LIMIT_BYTES = 56 * 1024 * 1024
ATTN_VMEM_BUDGET_BYTES = 50 * 1024 * 1024
TOKEN_TILE = 512
KEY_CHUNK = 256
MIN_DENOMINATOR = 2.0 ** -64
V_ONES_ROWS = 16
LANES = 128
KNORM_ROWS = 16
FF_CHUNKS = ((0, 1536), (1536, 1280))

_NT = (((1,), (1,)), ((), ()))


def _dot(a, b):
    return jnp.dot(a, b, preferred_element_type=F32)


def _dot_nt(a, b):
    return lax.dot_general(a, b, _NT, preferred_element_type=F32)


def _rms_rows(x, g):
    return x * lax.rsqrt(jnp.mean(x * x, axis=-1, keepdims=True) + EPS) * g


def _rms_cols(x, g):
    return x * lax.rsqrt(jnp.mean(x * x, axis=0, keepdims=True) + EPS) * g


def _tile_max(col):
    return jnp.broadcast_to(jnp.max(col, axis=0, keepdims=True), (1, LANES))


def _const_spec(shape):
    return pl.BlockSpec(shape, lambda *_: (0,) * len(shape), pipeline_mode=pl.Buffered(1))


def _ffn_kernel(*refs, has_wo, final_norm):
    refs = list(refs)
    x_ref = refs.pop(0)
    if has_wo:
        m_ref = refs.pop(0)
        wo_ref = refs.pop(0)
    g_ref, wa_ref, wb_ref, wout_ref = refs[:4]
    refs = refs[4:]
    if final_norm:
        gf_ref = refs.pop(0)
    (o_ref,) = refs

    x = x_ref[...]
    if has_wo:
        x = x + _dot(m_ref[...], wo_ref[...])
    xn = _rms_rows(x, g_ref[...]).astype(BF16)
    acc = None
    for start, size in FF_CHUNKS:
        a = _dot(xn, wa_ref[:, start:start + size])
        b = _dot(xn, wb_ref[:, start:start + size])
        h = (a * jax.nn.sigmoid(a) * b).astype(BF16)
        part = _dot(h, wout_ref[start:start + size, :])
        acc = part if acc is None else acc + part
    y = x + 0.5 * acc
    if final_norm:
        y = _rms_rows(y, gf_ref[...])
    o_ref[...] = y


def _ffn(x2d, gain, wa, wb, wout, merged=None, w_o=None, final_gain=None):
    n_tok, d = x2d.shape
    tm = min(TOKEN_TILE, n_tok)
    has_wo = merged is not None
    final_norm = final_gain is not None
    tile = pl.BlockSpec((tm, d), lambda i: (i, 0))
    args, specs = [x2d], [tile]
    if has_wo:
        args += [merged, w_o]
        specs += [tile, _const_spec(w_o.shape)]
    args += [gain, wa, wb, wout]
    specs += [_const_spec(gain.shape), _const_spec(wa.shape), _const_spec(wb.shape),
              _const_spec(wout.shape)]
    if final_norm:
        args.append(final_gain)
        specs.append(_const_spec(final_gain.shape))
    return pl.pallas_call(
        functools.partial(_ffn_kernel, has_wo=has_wo, final_norm=final_norm),
        grid=(n_tok // tm,),
        in_specs=specs,
        out_specs=tile,
        out_shape=jax.ShapeDtypeStruct((n_tok, d), F32),
        compiler_params=pltpu.CompilerParams(
            dimension_semantics=("arbitrary",), vmem_limit_bytes=VMEM_LIMIT_BYTES),
        name="ffn",
    )(*args)


def _proj_kernel(x_ref, gmix_ref, w1_ref, w2t_ref, gcq_ref, wuqt_ref, gckv_ref, wukvk_ref,
                 wukvvt_ref, gqn_ref, gkn_ref, ropek_a_ref, cosq_a_ref, sinq_a_ref,
                 cosk_b_ref, sink_b_ref, cosq_b_ref, sinq_b_ref,
                 qa_ref, ka_ref, va_ref, qb_ref, kb_ref, vb_ref, gate_ref, kn_ref):
    hb = _rms_rows(x_ref[...], gmix_ref[...]).astype(BF16)

    c = _dot(hb, w1_ref[...])
    o_ckv = Q_LORA
    o_kr = o_ckv + KV_LORA
    o_kb = o_kr + 2 * MLA_ROPE
    o_gate = o_kb + GQA_KV_HEADS * GQA_HD
    gate_ref[...] = jax.nn.sigmoid(c[:, o_gate:])

    cqn = _rms_rows(c[:, :Q_LORA], gcq_ref[...]).astype(BF16)
    ckvn = _rms_rows(c[:, o_ckv:o_kr], gckv_ref[...]).astype(BF16)

    kr2 = c[:, o_kr:o_kb] * ropek_a_ref[...]
    kr = (kr2 + pltpu.roll(kr2, MLA_ROPE, axis=1))[:, :MLA_ROPE]
    k_nope = _dot(ckvn, wukvk_ref[...])
    kr_norm2 = jnp.sum(kr * kr, axis=-1, keepdims=True)
    for h in range(MLA_HEADS):
        kh = k_nope[:, h * MLA_NOPE:(h + 1) * MLA_NOPE]
        ka_ref[h, :, :MLA_NOPE] = kh.astype(BF16)
        ka_ref[h, :, MLA_NOPE:] = kr.astype(BF16)
        kn_ref[h:h + 1, :] = _tile_max(jnp.sum(kh * kh, axis=-1, keepdims=True) + kr_norm2)

    for h in range(GQA_KV_HEADS):
        kb = _rms_rows(c[:, o_kb + h * GQA_HD:o_kb + (h + 1) * GQA_HD], gkn_ref[...])
        kb = kb * cosk_b_ref[...] + pltpu.roll(kb, GQA_HD // 2, axis=1) * sink_b_ref[...]
        kb_ref[h] = kb.astype(BF16)
        row = MLA_HEADS + h
        kn_ref[row:row + 1, :] = _tile_max(jnp.sum(kb * kb, axis=-1, keepdims=True))
    n_used = MLA_HEADS + GQA_KV_HEADS
    kn_ref[n_used:, :] = jnp.zeros((KNORM_ROWS - n_used, LANES), F32)

    ones = jnp.ones((V_ONES_ROWS, hb.shape[0]), BF16)
    va = _dot_nt(wukvvt_ref[...], ckvn)
    for h in range(MLA_HEADS):
        va_ref[h, :MLA_V] = va[h * MLA_V:(h + 1) * MLA_V].astype(BF16)
        va_ref[h, MLA_V:] = ones

    qa = _dot_nt(wuqt_ref[...], cqn)
    sa = MLA_QK ** -0.5 * LOG2E
    half = MLA_ROPE // 2
    cos_a, sin_a = cosq_a_ref[...], sinq_a_ref[...]
    for h in range(MLA_HEADS):
        base = h * MLA_QK
        x1 = qa[base + MLA_NOPE:base + MLA_NOPE + half]
        x2 = qa[base + MLA_NOPE + half:base + MLA_QK]
        qa_ref[h, :MLA_NOPE] = (qa[base:base + MLA_NOPE] * sa).astype(BF16)
        qa_ref[h, MLA_NOPE:MLA_NOPE + half] = ((x1 * cos_a - x2 * sin_a) * sa).astype(BF16)
        qa_ref[h, MLA_NOPE + half:] = ((x2 * cos_a + x1 * sin_a) * sa).astype(BF16)

    qv = _dot_nt(w2t_ref[...], hb)
    sb = GQA_HD ** -0.5 * LOG2E
    half = GQA_HD // 2
    cos_b, sin_b = cosq_b_ref[...], sinq_b_ref[...]
    for h in range(GQA_HEADS):
        q = _rms_cols(qv[h * GQA_HD:(h + 1) * GQA_HD], gqn_ref[...])
        x1, x2 = q[:half], q[half:]
        qb_ref[h, :half] = ((x1 * cos_b - x2 * sin_b) * sb).astype(BF16)
        qb_ref[h, half:] = ((x2 * cos_b + x1 * sin_b) * sb).astype(BF16)
    o_vb = GQA_HEADS * GQA_HD
    for h in range(GQA_KV_HEADS):
        vb_ref[h, :GQA_HD] = qv[o_vb + h * GQA_HD:o_vb + (h + 1) * GQA_HD].astype(BF16)
        vb_ref[h, GQA_HD:] = ones


def _proj(x, weights, tables):
    B, S, D = x.shape
    tm = min(TOKEN_TILE, S)
    nt = S // tm
    (gmix, w1, w2t, gcq, wuqt, gckv, wukvk, wukvvt, gqn, gkn) = weights
    ropek_a, cosq_a, sinq_a, cosk_b, sink_b, cosq_b, sinq_b = tables

    def tok_table(t):
        return pl.BlockSpec((tm, t.shape[1]), lambda b, i: (i, 0))

    def feat_table(t):
        return pl.BlockSpec((t.shape[0], tm), lambda b, i: (0, i))

    def feat_out(heads, width):
        return (jax.ShapeDtypeStruct((B, heads, nt, width, tm), BF16),
                pl.BlockSpec((None, heads, None, width, tm), lambda b, i: (b, 0, i, 0, 0)))

    def tok_out(heads, width):
        return (jax.ShapeDtypeStruct((B, heads, S, width), BF16),
                pl.BlockSpec((None, heads, tm, width), lambda b, i: (b, 0, i, 0)))

    outs = [feat_out(MLA_HEADS, MLA_QK), tok_out(MLA_HEADS, MLA_QK),
            feat_out(MLA_HEADS, MLA_V + V_ONES_ROWS),
            feat_out(GQA_HEADS, GQA_HD), tok_out(GQA_KV_HEADS, GQA_HD),
            feat_out(GQA_KV_HEADS, GQA_HD + V_ONES_ROWS),
            (jax.ShapeDtypeStruct((B, S, 2 * D), F32),
             pl.BlockSpec((None, tm, 2 * D), lambda b, i: (b, i, 0))),
            (jax.ShapeDtypeStruct((B, nt, KNORM_ROWS, LANES), F32),
             pl.BlockSpec((None, None, KNORM_ROWS, LANES), lambda b, i: (b, i, 0, 0)))]
    in_specs = [pl.BlockSpec((None, tm, D), lambda b, i: (b, i, 0))]
    in_specs += [_const_spec(w.shape) for w in weights]
    in_specs += [tok_table(ropek_a), feat_table(cosq_a), feat_table(sinq_a), tok_table(cosk_b),
                 tok_table(sink_b), feat_table(cosq_b), feat_table(sinq_b)]
    return pl.pallas_call(
        _proj_kernel,
        grid=(B, nt),
        in_specs=in_specs,
        out_specs=[o[1] for o in outs],
        out_shape=[o[0] for o in outs],
        compiler_params=pltpu.CompilerParams(
            dimension_semantics=("arbitrary", "arbitrary"), vmem_limit_bytes=VMEM_LIMIT_BYTES),
        name="proj",
    )(x, *weights, *tables)


def _attn_kernel(qa_ref, ka_ref, va_ref, qb_ref, kb_ref, vb_ref, ga_ref, gb_ref, kn_ref, o_ref,
                 s_ref, p_ref, acc_ref, lmin_ref):
    q_refs, k_refs, v_refs = (qa_ref, qb_ref), (ka_ref, kb_ref), (va_ref, vb_ref)
    n_tiles, _, tq = qa_ref.shape
    tm = va_ref.shape[2]
    tk = s_ref.shape[1]
    n_chunks = n_tiles * tm // tk
    dv = o_ref.shape[1]

    def key_norm2_max(row):
        per_head = jnp.max(kn_ref[...], axis=0)
        rows = lax.broadcasted_iota(jnp.int32, per_head.shape, 0)
        picked = jnp.max(jnp.where(rows == row, per_head, 0.0), axis=0, keepdims=True)
        return picked[:, :1]

    def scores(qt, c):
        return [_dot(k_refs[i][c * tk:(c + 1) * tk, :], q_refs[i][qt]) for i in range(2)]

    def values(c, p, first):
        start = c * tk
        for i in range(2):
            v = v_refs[i][start // tm][:, start % tm:start % tm + tk]
            pv = _dot(v, p[i])
            acc_ref[i] = pv if first else acc_ref[i] + pv

    def finalize(qt):
        rows = pl.ds(pl.multiple_of(qt * tq, tq), tq)
        o, l = [], []
        for i in range(2):
            acc = acc_ref[i]
            l.append(acc[dv:dv + 1])
            o.append((acc[:dv] / l[i]).T)
        lmin_ref[qt] = jnp.minimum(l[0], l[1])
        o_ref[rows, :] = (ga_ref[rows, :] * o[0] + gb_ref[rows, :] * o[1]).astype(BF16)

    def tile(qt, carry):
        shift = []
        for i in range(2):
            q = q_refs[i][qt].astype(F32)
            shift.append(jnp.sqrt(jnp.sum(q * q, axis=0, keepdims=True) * knorm2[i]))
        s = [s_ref[i] for i in range(2)]
        p_prev = [p_ref[i] for i in range(2)]
        for c in range(n_chunks):
            if c + 1 < n_chunks:
                s_next = scores(qt, c + 1)
            else:
                s_next = scores(jnp.minimum(qt + 1, n_tiles - 1), 0)
            p = [jnp.exp2(s[i] - shift[i]).astype(BF16) for i in range(2)]
            values((c - 1) % n_chunks, p_prev, first=(c == 1 or n_chunks == 1))
            if c == 0:
                finalize(jnp.maximum(qt - 1, 0))
            s, p_prev = s_next, p
        for i in range(2):
            s_ref[i] = s[i]
            p_ref[i] = p_prev[i]
        return carry

    def robust_tile(qt):
        for i in range(2):
            q = q_refs[i][qt]
            acc_ref[i] = jnp.zeros(acc_ref.shape[1:], F32)

            def body(c, m, i=i, q=q):
                k = k_refs[i][pl.ds(pl.multiple_of(c * tm, tm), tm), :]
                s = _dot(k, q)
                m_new = jnp.maximum(m, jnp.max(s, axis=0, keepdims=True))
                p = jnp.exp2(s - m_new).astype(BF16)
                acc_ref[i] = jnp.exp2(m - m_new) * acc_ref[i] + _dot(v_refs[i][c], p)
                return m_new

            lax.fori_loop(0, n_tiles, body, jnp.full((1, tq), -jnp.inf, F32))

    head = pl.program_id(1)
    knorm2 = [key_norm2_max(head), key_norm2_max(MLA_HEADS + head // GQA_GROUP)]
    acc_ref[...] = jnp.ones_like(acc_ref)
    p_ref[...] = jnp.zeros_like(p_ref)
    s0 = scores(0, 0)
    for i in range(2):
        s_ref[i] = s0[i]
    lax.fori_loop(0, n_tiles, tile, 0)
    values(n_chunks - 1, [p_ref[i] for i in range(2)], first=n_chunks == 1)
    finalize(n_tiles - 1)

    def redo_underflowed(qt, carry):
        @pl.when(jnp.logical_not(jnp.min(lmin_ref[qt]) >= MIN_DENOMINATOR))
        def _():
            robust_tile(qt)
            finalize(qt)
        return carry

    @pl.when(jnp.logical_not(jnp.min(lmin_ref[...]) >= MIN_DENOMINATOR))
    def _():
        lax.fori_loop(0, n_tiles, redo_underflowed, 0)


def _attn(qa, ka, va, qb, kb, vb, gates, knorm):
    B, H, nt, _, tm = qa.shape
    S = ka.shape[2]
    dv_ext = va.shape[3]

    def head_spec(arr, group, **kw):
        zeros = (0,) * (arr.ndim - 2)
        return pl.BlockSpec((None, None) + arr.shape[2:], lambda b, h: (b, h // group) + zeros, **kw)

    bytes_per_head = sum(math.prod(a.shape[2:]) * a.dtype.itemsize for a in (qa, ka, va, qb, kb, vb))
    tk = min(KEY_CHUNK, tm)
    scratch_bytes = 2 * tk * tm * (4 + 2) + 2 * dv_ext * tm * 4
    gate_bytes = 2 * S * MLA_V * 4
    out_bytes = S * MLA_V * 2
    double_buffered = 2 * (bytes_per_head + gate_bytes + out_bytes) + scratch_bytes
    gate_kw = {} if double_buffered <= ATTN_VMEM_BUDGET_BYTES else {"pipeline_mode": pl.Buffered(1)}

    in_specs = [
        head_spec(qa, 1), head_spec(ka, 1), head_spec(va, 1),
        head_spec(qb, 1), head_spec(kb, GQA_GROUP), head_spec(vb, GQA_GROUP),
        pl.BlockSpec((None, S, MLA_V), lambda b, h: (b, 0, h), **gate_kw),
        pl.BlockSpec((None, S, GQA_HD), lambda b, h: (b, 0, H + h), **gate_kw),
        pl.BlockSpec((None,) + knorm.shape[1:], lambda b, h: (b, 0, 0, 0)),
    ]
    return pl.pallas_call(
        _attn_kernel,
        grid=(B, H),
        in_specs=in_specs,
        out_specs=pl.BlockSpec((None, S, MLA_V), lambda b, h: (b, 0, h)),
        out_shape=jax.ShapeDtypeStruct((B, S, H * MLA_V), BF16),
        scratch_shapes=[pltpu.VMEM((2, tk, tm), F32),
                        pltpu.VMEM((2, tk, tm), BF16),
                        pltpu.VMEM((2, dv_ext, tm), F32),
                        pltpu.VMEM((nt, 1, tm), F32)],
        compiler_params=pltpu.CompilerParams(
            dimension_semantics=("arbitrary", "arbitrary"), vmem_limit_bytes=VMEM_LIMIT_BYTES),
        name="attn",
    )(qa, ka, va, qb, kb, vb, gates, gates, knorm)


def _rope_tables(S):
    rows = S // GRID_W
    row = jnp.repeat(jnp.arange(rows, dtype=F32), GRID_W)
    col = jnp.tile(jnp.arange(GRID_W, dtype=F32), rows)

    def cos_sin(rot_dim):
        n = rot_dim // 4
        freqs = ROPE_THETA ** (-jnp.arange(n, dtype=F32) / n)
        ang = jnp.concatenate([row[:, None] * freqs, col[:, None] * freqs], axis=-1)
        return jnp.cos(ang), jnp.sin(ang)

    cos_a, sin_a = cos_sin(MLA_ROPE)
    cos_b, sin_b = cos_sin(GQA_HD)
    ropek_a = jnp.concatenate([cos_a, cos_a, sin_a, sin_a], axis=-1)
    cosk_b = jnp.concatenate([cos_b, cos_b], axis=-1)
    sink_b = jnp.concatenate([-sin_b, sin_b], axis=-1)
    return (ropek_a, cos_a.T, sin_a.T, cosk_b, sink_b, cos_b.T, sin_b.T)


def _prep_layer(l, norm_ffn1, w_ffn1_in, w_ffn1_out, norm_mix, w_in, g_cq, w_uq, g_ckv, w_ukv,
                g_qn, g_kn, w_o, norm_ffn2, w_ffn2_in, w_ffn2_out):
    D = w_in.shape[1]

    def ffn_w(norm, wi, wo_):
        wa, wb = jnp.split(wi[l].astype(BF16), 2, axis=-1)
        return norm[l][None, :], wa, wb, wo_[l].astype(BF16)

    wi = w_in[l].astype(BF16)
    edges = [0]
    for w in (Q_LORA, KV_LORA, MLA_ROPE, GQA_HEADS * GQA_HD, GQA_KV_HEADS * GQA_HD,
              GQA_KV_HEADS * GQA_HD, D, D):
        edges.append(edges[-1] + w)
    c_q, c_kv, k_r, q_b, k_b, v_b, gate_a, gate_b = [wi[:, edges[i]:edges[i + 1]] for i in range(8)]
    half = MLA_ROPE // 2
    k_r_rot = jnp.concatenate([-k_r[:, half:], k_r[:, :half]], axis=-1)
    w1 = jnp.concatenate([c_q, c_kv, k_r, k_r_rot, k_b, gate_a, gate_b], axis=-1)
    w2t = jnp.concatenate([q_b, v_b], axis=-1).T
    wukv = w_ukv[l].astype(BF16).reshape(KV_LORA, MLA_HEADS, MLA_NOPE + MLA_V)
    wukvk = wukv[:, :, :MLA_NOPE].reshape(KV_LORA, MLA_HEADS * MLA_NOPE)
    wukvvt = wukv[:, :, MLA_NOPE:].reshape(KV_LORA, MLA_HEADS * MLA_V).T
    proj_w = (norm_mix[l][None, :], w1, w2t, g_cq[l][None, :], w_uq[l].astype(BF16).T,
              g_ckv[l][None, :], wukvk, wukvvt, g_qn[l][:, None], g_kn[l][None, :])
    return (ffn_w(norm_ffn1, w_ffn1_in, w_ffn1_out), proj_w, w_o[l].astype(BF16),
            ffn_w(norm_ffn2, w_ffn2_in, w_ffn2_out))


def _trunk(x, layers, norm_final):
    B, S, D = x.shape
    tables = _rope_tables(S)
    x2d = x.reshape(B * S, D)
    for i, (ffn1_w, proj_w, w_o, ffn2_w) in enumerate(layers):
        x2d = _ffn(x2d, *ffn1_w)
        qa, ka, va, qb, kb, vb, gates, knorm = _proj(x2d.reshape(B, S, D), proj_w, tables)
        merged = _attn(qa, ka, va, qb, kb, vb, gates, knorm)
        final_gain = norm_final[None, :] if i == len(layers) - 1 else None
        x2d = _ffn(x2d, *ffn2_w, merged=merged.reshape(B * S, D), w_o=w_o, final_gain=final_gain)
    return x2d.reshape(B, S, D)


def kernel(x_prompt, x_sample, norm_ffn1, w_ffn1_in, w_ffn1_out, norm_mix, w_in, g_cq, w_uq, g_ckv, w_ukv, g_qn, g_kn, w_o, norm_ffn2, w_ffn2_in, w_ffn2_out, norm_final):
    depth = w_in.shape[0]
    layers = [_prep_layer(l, norm_ffn1, w_ffn1_in, w_ffn1_out, norm_mix, w_in, g_cq, w_uq, g_ckv,
                          w_ukv, g_qn, g_kn, w_o, norm_ffn2, w_ffn2_in, w_ffn2_out)
              for l in range(depth)]
    return (_trunk(x_prompt, layers, norm_final), _trunk(x_sample, layers, norm_final))
```

```python
import functools
import math

import jax
import jax.numpy as jnp
from jax import lax
from jax.experimental import pallas as pl
from jax.experimental.pallas import tpu as pltpu

F32 = jnp.float32
BF16 = jnp.bfloat16

EPS = 1e-6
ROPE_THETA = 10000.0
GRID_W = 64

MLA_HEADS = 8
MLA_NOPE = 128
MLA_ROPE = 64
MLA_QK = MLA_NOPE + MLA_ROPE
MLA_V = 128
Q_LORA = 384
KV_LORA = 256
GQA_HEADS = 8
GQA_KV_HEADS = 2
GQA_GROUP = GQA_HEADS // GQA_KV_HEADS
GQA_HD = 128
D_FF = 2816

LOG2E = math.log2(math.e)

VMEM_LIMIT_BYTES = 56 * 1024 * 1024
ATTN_VMEM_BUDGET_BYTES = 50 * 1024 * 1024
TOKEN_TILE = 512
FFN_TOKEN_TILE = 1024
PROJ_SUBTILES = 2
KEY_CHUNK = 256
MIN_DENOMINATOR = 2.0 ** -64
V_ONES_ROWS = 16
LANES = 128
KNORM_ROWS = 16
FF_CHUNKS = ((0, 1536), (1536, 1280))

_NT = (((1,), (1,)), ((), ()))


def _dot(a, b):
    return jnp.dot(a, b, preferred_element_type=F32)


def _dot_nt(a, b):
    return lax.dot_general(a, b, _NT, preferred_element_type=F32)


def _rms_rows(x, g):
    return x * lax.rsqrt(jnp.mean(x * x, axis=-1, keepdims=True) + EPS) * g


def _rms_cols(x, g):
    return x * lax.rsqrt(jnp.mean(x * x, axis=0, keepdims=True) + EPS) * g


def _tile_max(col):
    return jnp.broadcast_to(jnp.max(col, axis=0, keepdims=True), (1, LANES))


def _const_spec(shape):
    return pl.BlockSpec(shape, lambda *_: (0,) * len(shape), pipeline_mode=pl.Buffered(1))


def _ffn_kernel(*refs, has_wo, final_norm):
    refs = list(refs)
    x_ref = refs.pop(0)
    if has_wo:
        m_ref = refs.pop(0)
        wo_ref = refs.pop(0)
    g_ref, wa_ref, wb_ref, wout_ref = refs[:4]
    refs = refs[4:]
    if final_norm:
        gf_ref = refs.pop(0)
    (o_ref,) = refs

    x = x_ref[...]
    if has_wo:
        x = x + _dot(m_ref[...], wo_ref[...])
    xn = _rms_rows(x, g_ref[...]).astype(BF16)
    acc = None
    for start, size in FF_CHUNKS:
        a = _dot(xn, wa_ref[:, start:start + size])
        b = _dot(xn, wb_ref[:, start:start + size])
        h = (a * jax.nn.sigmoid(a) * b).astype(BF16)
        part = _dot(h, wout_ref[start:start + size, :])
        acc = part if acc is None else acc + part
    y = x + 0.5 * acc
    if final_norm:
        y = _rms_rows(y, gf_ref[...])
    o_ref[...] = y


def _ffn(x2d, gain, wa, wb, wout, merged=None, w_o=None, final_gain=None):
    n_tok, d = x2d.shape
    tm = min(FFN_TOKEN_TILE, n_tok)
    has_wo = merged is not None
    final_norm = final_gain is not None
    tile = pl.BlockSpec((tm, d), lambda i: (i, 0))
    args, specs = [x2d], [tile]
    if has_wo:
        args += [merged, w_o]
        specs += [tile, _const_spec(w_o.shape)]
    args += [gain, wa, wb, wout]
    specs += [_const_spec(gain.shape), _const_spec(wa.shape), _const_spec(wb.shape),
              _const_spec(wout.shape)]
    if final_norm:
        args.append(final_gain)
        specs.append(_const_spec(final_gain.shape))
    return pl.pallas_call(
        functools.partial(_ffn_kernel, has_wo=has_wo, final_norm=final_norm),
        grid=(n_tok // tm,),
        in_specs=specs,
        out_specs=tile,
        out_shape=jax.ShapeDtypeStruct((n_tok, d), F32),
        compiler_params=pltpu.CompilerParams(
            dimension_semantics=("arbitrary",), vmem_limit_bytes=VMEM_LIMIT_BYTES),
        name="ffn",
    )(*args)


def _proj_kernel(x_ref, gmix_ref, w1_ref, w2t_ref, gcq_ref, wuqt_ref, gckv_ref, wukvk_ref,
                 wukvvt_ref, gqn_ref, gkn_ref, ropek_a_ref, cosq_a_ref, sinq_a_ref,
                 cosk_b_ref, sink_b_ref, cosq_b_ref, sinq_b_ref,
                 qa_ref, ka_ref, va_ref, qb_ref, kb_ref, vb_ref, gate_ref, kn_ref):
    tm = x_ref.shape[0]
    sub = tm // PROJ_SUBTILES if tm % (PROJ_SUBTILES * LANES) == 0 else tm
    o_ckv = Q_LORA
    o_kr = o_ckv + KV_LORA
    o_kb = o_kr + 2 * MLA_ROPE
    o_gate = o_kb + GQA_KV_HEADS * GQA_HD
    n_key_heads = MLA_HEADS + GQA_KV_HEADS
    knorm2 = [None] * n_key_heads

    def note_norm2(row, col):
        m = _tile_max(col)
        knorm2[row] = m if knorm2[row] is None else jnp.maximum(knorm2[row], m)

    for r0 in range(0, tm, sub):
        t = slice(r0, r0 + sub)
        hb = _rms_rows(x_ref[t, :], gmix_ref[...]).astype(BF16)

        c = _dot(hb, w1_ref[...])
        gate_ref[t, :] = jax.nn.sigmoid(c[:, o_gate:])

        cqn = _rms_rows(c[:, :Q_LORA], gcq_ref[...]).astype(BF16)
        ckvn = _rms_rows(c[:, o_ckv:o_kr], gckv_ref[...]).astype(BF16)

        kr2 = c[:, o_kr:o_kb] * ropek_a_ref[t, :]
        kr = (kr2 + pltpu.roll(kr2, MLA_ROPE, axis=1))[:, :MLA_ROPE]
        k_nope = _dot(ckvn, wukvk_ref[...])
        kr_norm2 = jnp.sum(kr * kr, axis=-1, keepdims=True)
        for h in range(MLA_HEADS):
            kh = k_nope[:, h * MLA_NOPE:(h + 1) * MLA_NOPE]
            ka_ref[h, t, :MLA_NOPE] = kh.astype(BF16)
            ka_ref[h, t, MLA_NOPE:] = kr.astype(BF16)
            note_norm2(h, jnp.sum(kh * kh, axis=-1, keepdims=True) + kr_norm2)

        for h in range(GQA_KV_HEADS):
            kb = _rms_rows(c[:, o_kb + h * GQA_HD:o_kb + (h + 1) * GQA_HD], gkn_ref[...])
            kb = kb * cosk_b_ref[t, :] + pltpu.roll(kb, GQA_HD // 2, axis=1) * sink_b_ref[t, :]
            kb_ref[h, t, :] = kb.astype(BF16)
            note_norm2(MLA_HEADS + h, jnp.sum(kb * kb, axis=-1, keepdims=True))

        ones = jnp.ones((V_ONES_ROWS, sub), BF16)
        va = _dot_nt(wukvvt_ref[...], ckvn)
        for h in range(MLA_HEADS):
            va_ref[h, :MLA_V, t] = va[h * MLA_V:(h + 1) * MLA_V].astype(BF16)
            va_ref[h, MLA_V:, t] = ones

        qa = _dot_nt(wuqt_ref[...], cqn)
        sa = MLA_QK ** -0.5 * LOG2E
        half = MLA_ROPE // 2
        cos_a, sin_a = cosq_a_ref[:, t], sinq_a_ref[:, t]
        for h in range(MLA_HEADS):
            base = h * MLA_QK
            x1 = qa[base + MLA_NOPE:base + MLA_NOPE + half]
            x2 = qa[base + MLA_NOPE + half:base + MLA_QK]
            qa_ref[h, :MLA_NOPE, t] = (qa[base:base + MLA_NOPE] * sa).astype(BF16)
            qa_ref[h, MLA_NOPE:MLA_NOPE + half, t] = ((x1 * cos_a - x2 * sin_a) * sa).astype(BF16)
            qa_ref[h, MLA_NOPE + half:, t] = ((x2 * cos_a + x1 * sin_a) * sa).astype(BF16)

        qv = _dot_nt(w2t_ref[...], hb)
        sb = GQA_HD ** -0.5 * LOG2E
        half = GQA_HD // 2
        cos_b, sin_b = cosq_b_ref[:, t], sinq_b_ref[:, t]
        for h in range(GQA_HEADS):
            q = _rms_cols(qv[h * GQA_HD:(h + 1) * GQA_HD], gqn_ref[...])
            x1, x2 = q[:half], q[half:]
            qb_ref[h, :half, t] = ((x1 * cos_b - x2 * sin_b) * sb).astype(BF16)
            qb_ref[h, half:, t] = ((x2 * cos_b + x1 * sin_b) * sb).astype(BF16)
        o_vb = GQA_HEADS * GQA_HD
        for h in range(GQA_KV_HEADS):
            vb_ref[h, :GQA_HD, t] = qv[o_vb + h * GQA_HD:o_vb + (h + 1) * GQA_HD].astype(BF16)
            vb_ref[h, GQA_HD:, t] = ones

    for row in range(n_key_heads):
        kn_ref[row:row + 1, :] = knorm2[row]
    kn_ref[n_key_heads:, :] = jnp.zeros((KNORM_ROWS - n_key_heads, LANES), F32)


def _proj(x, weights, tables):
    B, S, D = x.shape
    tm = min(TOKEN_TILE, S)
    nt = S // tm
    (gmix, w1, w2t, gcq, wuqt, gckv, wukvk, wukvvt, gqn, gkn) = weights
    ropek_a, cosq_a, sinq_a, cosk_b, sink_b, cosq_b, sinq_b = tables

    def tok_table(t):
        return pl.BlockSpec((tm, t.shape[1]), lambda b, i: (i, 0))

    def feat_table(t):
        return pl.BlockSpec((t.shape[0], tm), lambda b, i: (0, i))

    def feat_out(heads, width):
        return (jax.ShapeDtypeStruct((B, heads, nt, width, tm), BF16),
                pl.BlockSpec((None, heads, None, width, tm), lambda b, i: (b, 0, i, 0, 0)))

    def tok_out(heads, width):
        return (jax.ShapeDtypeStruct((B, heads, S, width), BF16),
                pl.BlockSpec((None, heads, tm, width), lambda b, i: (b, 0, i, 0)))

    outs = [feat_out(MLA_HEADS, MLA_QK), tok_out(MLA_HEADS, MLA_QK),
            feat_out(MLA_HEADS, MLA_V + V_ONES_ROWS),
            feat_out(GQA_HEADS, GQA_HD), tok_out(GQA_KV_HEADS, GQA_HD),
            feat_out(GQA_KV_HEADS, GQA_HD + V_ONES_ROWS),
            (jax.ShapeDtypeStruct((B, S, 2 * D), F32),
             pl.BlockSpec((None, tm, 2 * D), lambda b, i: (b, i, 0))),
            (jax.ShapeDtypeStruct((B, nt, KNORM_ROWS, LANES), F32),
             pl.BlockSpec((None, None, KNORM_ROWS, LANES), lambda b, i: (b, i, 0, 0)))]
    in_specs = [pl.BlockSpec((None, tm, D), lambda b, i: (b, i, 0))]
    in_specs += [_const_spec(w.shape) for w in weights]
    in_specs += [tok_table(ropek_a), feat_table(cosq_a), feat_table(sinq_a), tok_table(cosk_b),
                 tok_table(sink_b), feat_table(cosq_b), feat_table(sinq_b)]
    return pl.pallas_call(
        _proj_kernel,
        grid=(B, nt),
        in_specs=in_specs,
        out_specs=[o[1] for o in outs],
        out_shape=[o[0] for o in outs],
        compiler_params=pltpu.CompilerParams(
            dimension_semantics=("arbitrary", "arbitrary"), vmem_limit_bytes=VMEM_LIMIT_BYTES),
        name="proj",
    )(x, *weights, *tables)


def _attn_kernel(qa_ref, ka_ref, va_ref, qb_ref, kb_ref, vb_ref, ga_ref, gb_ref, kn_ref, o_ref,
                 s_ref, p_ref, acc_ref, lmin_ref):
    q_refs, k_refs, v_refs = (qa_ref, qb_ref), (ka_ref, kb_ref), (va_ref, vb_ref)
    n_tiles, _, tq = qa_ref.shape
    tm = va_ref.shape[2]
    tk = s_ref.shape[1]
    n_chunks = n_tiles * tm // tk
    dv = o_ref.shape[1]

    def key_norm2_max(row):
        per_head = jnp.max(kn_ref[...], axis=0)
        rows = lax.broadcasted_iota(jnp.int32, per_head.shape, 0)
        picked = jnp.max(jnp.where(rows == row, per_head, 0.0), axis=0, keepdims=True)
        return picked[:, :1]

    def scores(qt, c):
        return [_dot(k_refs[i][c * tk:(c + 1) * tk, :], q_refs[i][qt]) for i in range(2)]

    def values(c, p, first):
        start = c * tk
        for i in range(2):
            v = v_refs[i][start // tm][:, start % tm:start % tm + tk]
            pv = _dot(v, p[i])
            acc_ref[i] = pv if first else acc_ref[i] + pv

    def finalize(qt):
        rows = pl.ds(pl.multiple_of(qt * tq, tq), tq)
        o, l = [], []
        for i in range(2):
            acc = acc_ref[i]
            l.append(acc[dv:dv + 1])
            o.append((acc[:dv] / l[i]).T)
        lmin_ref[qt] = jnp.minimum(l[0], l[1])
        o_ref[rows, :] = (ga_ref[rows, :] * o[0] + gb_ref[rows, :] * o[1]).astype(BF16)

    def tile(qt, carry):
        shift = []
        for i in range(2):
            q = q_refs[i][qt].astype(F32)
            shift.append(jnp.sqrt(jnp.sum(q * q, axis=0, keepdims=True) * knorm2[i]))
        s = [s_ref[i] for i in range(2)]
        p_prev = [p_ref[i] for i in range(2)]
        for c in range(n_chunks):
            if c + 1 < n_chunks:
                s_next = scores(qt, c + 1)
            else:
                s_next = scores(jnp.minimum(qt + 1, n_tiles - 1), 0)
            p = [jnp.exp2(s[i] - shift[i]).astype(BF16) for i in range(2)]
            values((c - 1) % n_chunks, p_prev, first=(c == 1 or n_chunks == 1))
            if c == 0:
                finalize(jnp.maximum(qt - 1, 0))
            s, p_prev = s_next, p
        for i in range(2):
            s_ref[i] = s[i]
            p_ref[i] = p_prev[i]
        return carry

    def robust_tile(qt):
        for i in range(2):
            q = q_refs[i][qt]
            acc_ref[i] = jnp.zeros(acc_ref.shape[1:], F32)

            def body(c, m, i=i, q=q):
                k = k_refs[i][pl.ds(pl.multiple_of(c * tm, tm), tm), :]
                s = _dot(k, q)
                m_new = jnp.maximum(m, jnp.max(s, axis=0, keepdims=True))
                p = jnp.exp2(s - m_new).astype(BF16)
                acc_ref[i] = jnp.exp2(m - m_new) * acc_ref[i] + _dot(v_refs[i][c], p)
                return m_new

            lax.fori_loop(0, n_tiles, body, jnp.full((1, tq), -jnp.inf, F32))

    head = pl.program_id(1)
    knorm2 = [key_norm2_max(head), key_norm2_max(MLA_HEADS + head // GQA_GROUP)]
    acc_ref[...] = jnp.ones_like(acc_ref)
    p_ref[...] = jnp.zeros_like(p_ref)
    s0 = scores(0, 0)
    for i in range(2):
        s_ref[i] = s0[i]
    lax.fori_loop(0, n_tiles, tile, 0)
    values(n_chunks - 1, [p_ref[i] for i in range(2)], first=n_chunks == 1)
    finalize(n_tiles - 1)

    def redo_underflowed(qt, carry):
        @pl.when(jnp.logical_not(jnp.min(lmin_ref[qt]) >= MIN_DENOMINATOR))
        def _():
            robust_tile(qt)
            finalize(qt)
        return carry

    @pl.when(jnp.logical_not(jnp.min(lmin_ref[...]) >= MIN_DENOMINATOR))
    def _():
        lax.fori_loop(0, n_tiles, redo_underflowed, 0)


def _attn(qa, ka, va, qb, kb, vb, gates, knorm):
    B, H, nt, _, tm = qa.shape
    S = ka.shape[2]
    dv_ext = va.shape[3]

    def head_spec(arr, group, **kw):
        zeros = (0,) * (arr.ndim - 2)
        return pl.BlockSpec((None, None) + arr.shape[2:], lambda b, h: (b, h // group) + zeros, **kw)

    bytes_per_head = sum(math.prod(a.shape[2:]) * a.dtype.itemsize for a in (qa, ka, va, qb, kb, vb))
    tk = min(KEY_CHUNK, tm)
    scratch_bytes = 2 * tk * tm * (4 + 2) + 2 * dv_ext * tm * 4
    gate_bytes = 2 * S * MLA_V * 4
    out_bytes = S * MLA_V * 2
    double_buffered = 2 * (bytes_per_head + gate_bytes + out_bytes) + scratch_bytes
    gate_kw = {} if double_buffered <= ATTN_VMEM_BUDGET_BYTES else {"pipeline_mode": pl.Buffered(1)}

    in_specs = [
        head_spec(qa, 1), head_spec(ka, 1), head_spec(va, 1),
        head_spec(qb, 1), head_spec(kb, GQA_GROUP), head_spec(vb, GQA_GROUP),
        pl.BlockSpec((None, S, MLA_V), lambda b, h: (b, 0, h), **gate_kw),
        pl.BlockSpec((None, S, GQA_HD), lambda b, h: (b, 0, H + h), **gate_kw),
        pl.BlockSpec((None,) + knorm.shape[1:], lambda b, h: (b, 0, 0, 0)),
    ]
    return pl.pallas_call(
        _attn_kernel,
        grid=(B, H),
        in_specs=in_specs,
        out_specs=pl.BlockSpec((None, S, MLA_V), lambda b, h: (b, 0, h)),
        out_shape=jax.ShapeDtypeStruct((B, S, H * MLA_V), BF16),
        scratch_shapes=[pltpu.VMEM((2, tk, tm), F32),
                        pltpu.VMEM((2, tk, tm), BF16),
                        pltpu.VMEM((2, dv_ext, tm), F32),
                        pltpu.VMEM((nt, 1, tm), F32)],
        compiler_params=pltpu.CompilerParams(
            dimension_semantics=("arbitrary", "arbitrary"), vmem_limit_bytes=VMEM_LIMIT_BYTES),
        name="attn",
    )(qa, ka, va, qb, kb, vb, gates, gates, knorm)


def _rope_tables(S):
    rows = S // GRID_W
    row = jnp.repeat(jnp.arange(rows, dtype=F32), GRID_W)
    col = jnp.tile(jnp.arange(GRID_W, dtype=F32), rows)

    def cos_sin(rot_dim):
        n = rot_dim // 4
        freqs = ROPE_THETA ** (-jnp.arange(n, dtype=F32) / n)
        ang = jnp.concatenate([row[:, None] * freqs, col[:, None] * freqs], axis=-1)
        return jnp.cos(ang), jnp.sin(ang)

    cos_a, sin_a = cos_sin(MLA_ROPE)
    cos_b, sin_b = cos_sin(GQA_HD)
    ropek_a = jnp.concatenate([cos_a, cos_a, sin_a, sin_a], axis=-1)
    cosk_b = jnp.concatenate([cos_b, cos_b], axis=-1)
    sink_b = jnp.concatenate([-sin_b, sin_b], axis=-1)
    return (ropek_a, cos_a.T, sin_a.T, cosk_b, sink_b, cos_b.T, sin_b.T)


def _prep_layer(l, norm_ffn1, w_ffn1_in, w_ffn1_out, norm_mix, w_in, g_cq, w_uq, g_ckv, w_ukv,
                g_qn, g_kn, w_o, norm_ffn2, w_ffn2_in, w_ffn2_out):
    D = w_in.shape[1]

    def ffn_w(norm, wi, wo_):
        wa, wb = jnp.split(wi[l].astype(BF16), 2, axis=-1)
        return norm[l][None, :], wa, wb, wo_[l].astype(BF16)

    wi = w_in[l].astype(BF16)
    edges = [0]
    for w in (Q_LORA, KV_LORA, MLA_ROPE, GQA_HEADS * GQA_HD, GQA_KV_HEADS * GQA_HD,
              GQA_KV_HEADS * GQA_HD, D, D):
        edges.append(edges[-1] + w)
    c_q, c_kv, k_r, q_b, k_b, v_b, gate_a, gate_b = [wi[:, edges[i]:edges[i + 1]] for i in range(8)]
    half = MLA_ROPE // 2
    k_r_rot = jnp.concatenate([-k_r[:, half:], k_r[:, :half]], axis=-1)
    w1 = jnp.concatenate([c_q, c_kv, k_r, k_r_rot, k_b, gate_a, gate_b], axis=-1)
    w2t = jnp.concatenate([q_b, v_b], axis=-1).T
    wukv = w_ukv[l].astype(BF16).reshape(KV_LORA, MLA_HEADS, MLA_NOPE + MLA_V)
    wukvk = wukv[:, :, :MLA_NOPE].reshape(KV_LORA, MLA_HEADS * MLA_NOPE)
    wukvvt = wukv[:, :, MLA_NOPE:].reshape(KV_LORA, MLA_HEADS * MLA_V).T
    proj_w = (norm_mix[l][None, :], w1, w2t, g_cq[l][None, :], w_uq[l].astype(BF16).T,
              g_ckv[l][None, :], wukvk, wukvvt, g_qn[l][:, None], g_kn[l][None, :])
    return (ffn_w(norm_ffn1, w_ffn1_in, w_ffn1_out), proj_w, w_o[l].astype(BF16),
            ffn_w(norm_ffn2, w_ffn2_in, w_ffn2_out))


def _trunk(x, layers, norm_final):
    B, S, D = x.shape
    tables = _rope_tables(S)
    x2d = x.reshape(B * S, D)
    for i, (ffn1_w, proj_w, w_o, ffn2_w) in enumerate(layers):
        x2d = _ffn(x2d, *ffn1_w)
        qa, ka, va, qb, kb, vb, gates, knorm = _proj(x2d.reshape(B, S, D), proj_w, tables)
        merged = _attn(qa, ka, va, qb, kb, vb, gates, knorm)
        final_gain = norm_final[None, :] if i == len(layers) - 1 else None
        x2d = _ffn(x2d, *ffn2_w, merged=merged.reshape(B * S, D), w_o=w_o, final_gain=final_gain)
    return x2d.reshape(B, S, D)


def kernel(x_prompt, x_sample, norm_ffn1, w_ffn1_in, w_ffn1_out, norm_mix, w_in, g_cq, w_uq, g_ckv, w_ukv, g_qn, g_kn, w_o, norm_ffn2, w_ffn2_in, w_ffn2_out, norm_final):
    depth = w_in.shape[0]
    layers = [_prep_layer(l, norm_ffn1, w_ffn1_in, w_ffn1_out, norm_mix, w_in, g_cq, w_uq, g_ckv,
                          w_ukv, g_qn, g_kn, w_o, norm_ffn2, w_ffn2_in, w_ffn2_out)
              for l in range(depth)]
    return (_trunk(x_prompt, layers, norm_final), _trunk(x_sample, layers, norm_final))
```

```python
import functools
import math

import jax
import jax.numpy as jnp
from jax import lax
from jax.experimental import pallas as pl
from jax.experimental.pallas import tpu as pltpu

F32 = jnp.float32
BF16 = jnp.bfloat16

EPS = 1e-6
ROPE_THETA = 10000.0
GRID_W = 64

MLA_HEADS = 8
MLA_NOPE = 128
MLA_ROPE = 64
MLA_QK = MLA_NOPE + MLA_ROPE
MLA_V = 128
Q_LORA = 384
KV_LORA = 256
GQA_HEADS = 8
GQA_KV_HEADS = 2
GQA_GROUP = GQA_HEADS // GQA_KV_HEADS
GQA_HD = 128
D_FF = 2816

LOG2E = math.log2(math.e)

VMEM_LIMIT_BYTES = 56 * 1024 * 1024
ATTN_VMEM_BUDGET_BYTES = 52 * 1024 * 1024
TOKEN_TILE = 512
FFN_TOKEN_TILE = 1024
PROJ_SUBTILES = 2
KEY_CHUNK = 256
MIN_DENOMINATOR = 2.0 ** -64
V_ONES_ROWS = 16
LANES = 128
KNORM_ROWS = 16
FF_CHUNKS = ((0, 1536), (1536, 1280))

_NT = (((1,), (1,)), ((), ()))


def _dot(a, b):
    return jnp.dot(a, b, preferred_element_type=F32)


def _dot_nt(a, b):
    return lax.dot_general(a, b, _NT, preferred_element_type=F32)


def _rms_rows(x, g):
    return x * lax.rsqrt(jnp.mean(x * x, axis=-1, keepdims=True) + EPS) * g


def _rms_cols(x, g):
    return x * lax.rsqrt(jnp.mean(x * x, axis=0, keepdims=True) + EPS) * g


def _tile_max(col):
    return jnp.broadcast_to(jnp.max(col, axis=0, keepdims=True), (1, LANES))


def _const_spec(shape):
    return pl.BlockSpec(shape, lambda *_: (0,) * len(shape), pipeline_mode=pl.Buffered(1))


def _ffn_kernel(*refs, has_wo, final_norm):
    refs = list(refs)
    x_ref = refs.pop(0)
    if has_wo:
        m_ref = refs.pop(0)
        wo_ref = refs.pop(0)
    g_ref, wa_ref, wb_ref, wout_ref = refs[:4]
    refs = refs[4:]
    if final_norm:
        gf_ref = refs.pop(0)
    (o_ref,) = refs

    x = x_ref[...]
    if has_wo:
        x = x + _dot(m_ref[...], wo_ref[...])
    xn = _rms_rows(x, g_ref[...]).astype(BF16)
    acc = None
    for start, size in FF_CHUNKS:
        a = _dot(xn, wa_ref[:, start:start + size])
        b = _dot(xn, wb_ref[:, start:start + size])
        h = (a * jax.nn.sigmoid(a) * b).astype(BF16)
        part = _dot(h, wout_ref[start:start + size, :])
        acc = part if acc is None else acc + part
    y = x + 0.5 * acc
    if final_norm:
        y = _rms_rows(y, gf_ref[...])
    o_ref[...] = y


def _ffn(x2d, gain, wa, wb, wout, merged=None, w_o=None, final_gain=None):
    n_tok, d = x2d.shape
    tm = min(FFN_TOKEN_TILE, n_tok)
    has_wo = merged is not None
    final_norm = final_gain is not None
    tile = pl.BlockSpec((tm, d), lambda i: (i, 0))
    args, specs = [x2d], [tile]
    if has_wo:
        args += [merged, w_o]
        specs += [tile, _const_spec(w_o.shape)]
    args += [gain, wa, wb, wout]
    specs += [_const_spec(gain.shape), _const_spec(wa.shape), _const_spec(wb.shape),
              _const_spec(wout.shape)]
    if final_norm:
        args.append(final_gain)
        specs.append(_const_spec(final_gain.shape))
    return pl.pallas_call(
        functools.partial(_ffn_kernel, has_wo=has_wo, final_norm=final_norm),
        grid=(n_tok // tm,),
        in_specs=specs,
        out_specs=tile,
        out_shape=jax.ShapeDtypeStruct((n_tok, d), F32),
        compiler_params=pltpu.CompilerParams(
            dimension_semantics=("arbitrary",), vmem_limit_bytes=VMEM_LIMIT_BYTES),
        name="ffn",
    )(*args)


def _proj_kernel(x_ref, gmix_ref, w1_ref, w2t_ref, gcq_ref, wuqt_ref, gckv_ref, wukvk_ref,
                 wukvvt_ref, gqn_ref, gkn_ref, ropek_a_ref, cosq_a_ref, sinq_a_ref,
                 cosk_b_ref, sink_b_ref, cosq_b_ref, sinq_b_ref,
                 qa_ref, ka_ref, va_ref, qb_ref, kb_ref, vb_ref, gate_ref, kn_ref):
    tm = x_ref.shape[0]
    sub = tm // PROJ_SUBTILES if tm % (PROJ_SUBTILES * LANES) == 0 else tm
    o_ckv = Q_LORA
    o_kr = o_ckv + KV_LORA
    o_kb = o_kr + 2 * MLA_ROPE
    o_gate = o_kb + GQA_KV_HEADS * GQA_HD
    n_key_heads = MLA_HEADS + GQA_KV_HEADS
    knorm2 = [None] * n_key_heads

    def note_norm2(row, col):
        m = _tile_max(col)
        knorm2[row] = m if knorm2[row] is None else jnp.maximum(knorm2[row], m)

    for r0 in range(0, tm, sub):
        t = slice(r0, r0 + sub)
        hb = _rms_rows(x_ref[t, :], gmix_ref[...]).astype(BF16)

        c = _dot(hb, w1_ref[...])
        gate_ref[t, :] = jax.nn.sigmoid(c[:, o_gate:])

        cqn = _rms_rows(c[:, :Q_LORA], gcq_ref[...]).astype(BF16)
        ckvn = _rms_rows(c[:, o_ckv:o_kr], gckv_ref[...]).astype(BF16)

        kr2 = c[:, o_kr:o_kb] * ropek_a_ref[t, :]
        kr = (kr2 + pltpu.roll(kr2, MLA_ROPE, axis=1))[:, :MLA_ROPE]
        k_nope = _dot(ckvn, wukvk_ref[...])
        kr_norm2 = jnp.sum(kr * kr, axis=-1, keepdims=True)
        for h in range(MLA_HEADS):
            kh = k_nope[:, h * MLA_NOPE:(h + 1) * MLA_NOPE]
            ka_ref[h, t, :MLA_NOPE] = kh.astype(BF16)
            ka_ref[h, t, MLA_NOPE:] = kr.astype(BF16)
            note_norm2(h, jnp.sum(kh * kh, axis=-1, keepdims=True) + kr_norm2)

        for h in range(GQA_KV_HEADS):
            kb = _rms_rows(c[:, o_kb + h * GQA_HD:o_kb + (h + 1) * GQA_HD], gkn_ref[...])
            kb = kb * cosk_b_ref[t, :] + pltpu.roll(kb, GQA_HD // 2, axis=1) * sink_b_ref[t, :]
            kb_ref[h, t, :] = kb.astype(BF16)
            note_norm2(MLA_HEADS + h, jnp.sum(kb * kb, axis=-1, keepdims=True))

        ones = jnp.ones((V_ONES_ROWS, sub), BF16)
        va = _dot_nt(wukvvt_ref[...], ckvn)
        for h in range(MLA_HEADS):
            va_ref[h, :MLA_V, t] = va[h * MLA_V:(h + 1) * MLA_V].astype(BF16)
            va_ref[h, MLA_V:, t] = ones

        qa = _dot_nt(wuqt_ref[...], cqn)
        sa = MLA_QK ** -0.5 * LOG2E
        half = MLA_ROPE // 2
        cos_a, sin_a = cosq_a_ref[:, t], sinq_a_ref[:, t]
        for h in range(MLA_HEADS):
            base = h * MLA_QK
            x1 = qa[base + MLA_NOPE:base + MLA_NOPE + half]
            x2 = qa[base + MLA_NOPE + half:base + MLA_QK]
            qa_ref[h, :MLA_NOPE, t] = (qa[base:base + MLA_NOPE] * sa).astype(BF16)
            qa_ref[h, MLA_NOPE:MLA_NOPE + half, t] = ((x1 * cos_a - x2 * sin_a) * sa).astype(BF16)
            qa_ref[h, MLA_NOPE + half:, t] = ((x2 * cos_a + x1 * sin_a) * sa).astype(BF16)

        qv = _dot_nt(w2t_ref[...], hb)
        sb = GQA_HD ** -0.5 * LOG2E
        half = GQA_HD // 2
        cos_b, sin_b = cosq_b_ref[:, t], sinq_b_ref[:, t]
        for h in range(GQA_HEADS):
            q = _rms_cols(qv[h * GQA_HD:(h + 1) * GQA_HD], gqn_ref[...])
            x1, x2 = q[:half], q[half:]
            qb_ref[h, :half, t] = ((x1 * cos_b - x2 * sin_b) * sb).astype(BF16)
            qb_ref[h, half:, t] = ((x2 * cos_b + x1 * sin_b) * sb).astype(BF16)
        o_vb = GQA_HEADS * GQA_HD
        for h in range(GQA_KV_HEADS):
            vb_ref[h, :GQA_HD, t] = qv[o_vb + h * GQA_HD:o_vb + (h + 1) * GQA_HD].astype(BF16)
            vb_ref[h, GQA_HD:, t] = ones

    for row in range(n_key_heads):
        kn_ref[row:row + 1, :] = knorm2[row]
    kn_ref[n_key_heads:, :] = jnp.zeros((KNORM_ROWS - n_key_heads, LANES), F32)


def _proj(x, weights, tables):
    B, S, D = x.shape
    tm = min(TOKEN_TILE, S)
    nt = S // tm
    (gmix, w1, w2t, gcq, wuqt, gckv, wukvk, wukvvt, gqn, gkn) = weights
    ropek_a, cosq_a, sinq_a, cosk_b, sink_b, cosq_b, sinq_b = tables

    def tok_table(t):
        return pl.BlockSpec((tm, t.shape[1]), lambda b, i: (i, 0))

    def feat_table(t):
        return pl.BlockSpec((t.shape[0], tm), lambda b, i: (0, i))

    def feat_out(heads, width):
        return (jax.ShapeDtypeStruct((B, heads, nt, width, tm), BF16),
                pl.BlockSpec((None, heads, None, width, tm), lambda b, i: (b, 0, i, 0, 0)))

    def tok_out(heads, width):
        return (jax.ShapeDtypeStruct((B, heads, S, width), BF16),
                pl.BlockSpec((None, heads, tm, width), lambda b, i: (b, 0, i, 0)))

    outs = [feat_out(MLA_HEADS, MLA_QK), tok_out(MLA_HEADS, MLA_QK),
            feat_out(MLA_HEADS, MLA_V + V_ONES_ROWS),
            feat_out(GQA_HEADS, GQA_HD), tok_out(GQA_KV_HEADS, GQA_HD),
            feat_out(GQA_KV_HEADS, GQA_HD + V_ONES_ROWS),
            (jax.ShapeDtypeStruct((B, S, 2 * D), F32),
             pl.BlockSpec((None, tm, 2 * D), lambda b, i: (b, i, 0))),
            (jax.ShapeDtypeStruct((B, nt, KNORM_ROWS, LANES), F32),
             pl.BlockSpec((None, None, KNORM_ROWS, LANES), lambda b, i: (b, i, 0, 0)))]
    in_specs = [pl.BlockSpec((None, tm, D), lambda b, i: (b, i, 0))]
    in_specs += [_const_spec(w.shape) for w in weights]
    in_specs += [tok_table(ropek_a), feat_table(cosq_a), feat_table(sinq_a), tok_table(cosk_b),
                 tok_table(sink_b), feat_table(cosq_b), feat_table(sinq_b)]
    return pl.pallas_call(
        _proj_kernel,
        grid=(B, nt),
        in_specs=in_specs,
        out_specs=[o[1] for o in outs],
        out_shape=[o[0] for o in outs],
        compiler_params=pltpu.CompilerParams(
            dimension_semantics=("arbitrary", "arbitrary"), vmem_limit_bytes=VMEM_LIMIT_BYTES),
        name="proj",
    )(x, *weights, *tables)


def _attn_kernel(qa_ref, ka_ref, va_ref, qb_ref, kb_ref, vb_ref, ga_ref, gb_ref, kn_ref, o_ref,
                 s_ref, p_ref, acc_ref, lmin_ref):
    q_refs, k_refs, v_refs = (qa_ref, qb_ref), (ka_ref, kb_ref), (va_ref, vb_ref)
    n_tiles, _, tq = qa_ref.shape
    tm = va_ref.shape[2]
    tk = s_ref.shape[1]
    n_chunks = n_tiles * tm // tk
    dv = o_ref.shape[1]

    def key_norm2_max(row):
        per_head = jnp.max(kn_ref[...], axis=0)
        rows = lax.broadcasted_iota(jnp.int32, per_head.shape, 0)
        picked = jnp.max(jnp.where(rows == row, per_head, 0.0), axis=0, keepdims=True)
        return picked[:, :1]

    def scores(qt, c):
        return [_dot(k_refs[i][c * tk:(c + 1) * tk, :], q_refs[i][qt]) for i in range(2)]

    def values(c, p, first):
        start = c * tk
        for i in range(2):
            v = v_refs[i][start // tm][:, start % tm:start % tm + tk]
            pv = _dot(v, p[i])
            acc_ref[i] = pv if first else acc_ref[i] + pv

    def finalize(qt):
        rows = pl.ds(pl.multiple_of(qt * tq, tq), tq)
        o, l = [], []
        for i in range(2):
            acc = acc_ref[i]
            l.append(acc[dv:dv + 1])
            o.append((acc[:dv] / l[i]).T)
        lmin_ref[qt] = jnp.minimum(l[0], l[1])
        o_ref[rows, :] = (ga_ref[rows, :] * o[0] + gb_ref[rows, :] * o[1]).astype(BF16)

    def tile(qt, carry):
        shift = []
        for i in range(2):
            q = q_refs[i][qt].astype(F32)
            shift.append(jnp.sqrt(jnp.sum(q * q, axis=0, keepdims=True) * knorm2[i]))
        s = [s_ref[i] for i in range(2)]
        p_prev = [p_ref[i] for i in range(2)]
        for c in range(n_chunks):
            if c + 1 < n_chunks:
                s_next = scores(qt, c + 1)
            else:
                s_next = scores(jnp.minimum(qt + 1, n_tiles - 1), 0)
            p = [jnp.exp2(s[i] - shift[i]).astype(BF16) for i in range(2)]
            values((c - 1) % n_chunks, p_prev, first=(c == 1 or n_chunks == 1))
            if c == 0:
                finalize(jnp.maximum(qt - 1, 0))
            s, p_prev = s_next, p
        for i in range(2):
            s_ref[i] = s[i]
            p_ref[i] = p_prev[i]
        return carry

    def robust_tile(qt):
        for i in range(2):
            q = q_refs[i][qt]
            acc_ref[i] = jnp.zeros(acc_ref.shape[1:], F32)

            def body(c, m, i=i, q=q):
                k = k_refs[i][pl.ds(pl.multiple_of(c * tm, tm), tm), :]
                s = _dot(k, q)
                m_new = jnp.maximum(m, jnp.max(s, axis=0, keepdims=True))
                p = jnp.exp2(s - m_new).astype(BF16)
                acc_ref[i] = jnp.exp2(m - m_new) * acc_ref[i] + _dot(v_refs[i][c], p)
                return m_new

            lax.fori_loop(0, n_tiles, body, jnp.full((1, tq), -jnp.inf, F32))

    head = pl.program_id(1)
    knorm2 = [key_norm2_max(head), key_norm2_max(MLA_HEADS + head // GQA_GROUP)]
    acc_ref[...] = jnp.ones_like(acc_ref)
    p_ref[...] = jnp.zeros_like(p_ref)
    s0 = scores(0, 0)
    for i in range(2):
        s_ref[i] = s0[i]
    lax.fori_loop(0, n_tiles, tile, 0)
    values(n_chunks - 1, [p_ref[i] for i in range(2)], first=n_chunks == 1)
    finalize(n_tiles - 1)

    def redo_underflowed(qt, carry):
        @pl.when(jnp.logical_not(jnp.min(lmin_ref[qt]) >= MIN_DENOMINATOR))
        def _():
            robust_tile(qt)
            finalize(qt)
        return carry

    @pl.when(jnp.logical_not(jnp.min(lmin_ref[...]) >= MIN_DENOMINATOR))
    def _():
        lax.fori_loop(0, n_tiles, redo_underflowed, 0)


def _attn(qa, ka, va, qb, kb, vb, gates, knorm):
    B, H, nt, _, tm = qa.shape
    S = ka.shape[2]
    dv_ext = va.shape[3]

    def head_spec(arr, group, **kw):
        zeros = (0,) * (arr.ndim - 2)
        return pl.BlockSpec((None, None) + arr.shape[2:], lambda b, h: (b, h // group) + zeros, **kw)

    bytes_per_head = sum(math.prod(a.shape[2:]) * a.dtype.itemsize for a in (qa, ka, va, qb, kb, vb))
    tk = min(KEY_CHUNK, tm)
    scratch_bytes = 2 * tk * tm * (4 + 2) + 2 * dv_ext * tm * 4
    gate_bytes = 2 * S * MLA_V * 4
    out_bytes = S * MLA_V * 2
    double_buffered = 2 * (bytes_per_head + gate_bytes + out_bytes) + scratch_bytes
    gate_kw = {} if double_buffered <= ATTN_VMEM_BUDGET_BYTES else {"pipeline_mode": pl.Buffered(1)}

    in_specs = [
        head_spec(qa, 1), head_spec(ka, 1), head_spec(va, 1),
        head_spec(qb, 1), head_spec(kb, GQA_GROUP), head_spec(vb, GQA_GROUP),
        pl.BlockSpec((None, S, MLA_V), lambda b, h: (b, 0, h), **gate_kw),
        pl.BlockSpec((None, S, GQA_HD), lambda b, h: (b, 0, H + h), **gate_kw),
        pl.BlockSpec((None,) + knorm.shape[1:], lambda b, h: (b, 0, 0, 0)),
    ]
    return pl.pallas_call(
        _attn_kernel,
        grid=(B, H),
        in_specs=in_specs,
        out_specs=pl.BlockSpec((None, S, MLA_V), lambda b, h: (b, 0, h)),
        out_shape=jax.ShapeDtypeStruct((B, S, H * MLA_V), BF16),
        scratch_shapes=[pltpu.VMEM((2, tk, tm), F32),
                        pltpu.VMEM((2, tk, tm), BF16),
                        pltpu.VMEM((2, dv_ext, tm), F32),
                        pltpu.VMEM((nt, 1, tm), F32)],
        compiler_params=pltpu.CompilerParams(
            dimension_semantics=("arbitrary", "arbitrary"), vmem_limit_bytes=VMEM_LIMIT_BYTES),
        name="attn",
    )(qa, ka, va, qb, kb, vb, gates, gates, knorm)


def _rope_tables(S):
    rows = S // GRID_W
    row = jnp.repeat(jnp.arange(rows, dtype=F32), GRID_W)
    col = jnp.tile(jnp.arange(GRID_W, dtype=F32), rows)

    def cos_sin(rot_dim):
        n = rot_dim // 4
        freqs = ROPE_THETA ** (-jnp.arange(n, dtype=F32) / n)
        ang = jnp.concatenate([row[:, None] * freqs, col[:, None] * freqs], axis=-1)
        return jnp.cos(ang), jnp.sin(ang)

    cos_a, sin_a = cos_sin(MLA_ROPE)
    cos_b, sin_b = cos_sin(GQA_HD)
    ropek_a = jnp.concatenate([cos_a, cos_a, sin_a, sin_a], axis=-1)
    cosk_b = jnp.concatenate([cos_b, cos_b], axis=-1)
    sink_b = jnp.concatenate([-sin_b, sin_b], axis=-1)
    return (ropek_a, cos_a.T, sin_a.T, cosk_b, sink_b, cos_b.T, sin_b.T)


def _prep_layer(l, norm_ffn1, w_ffn1_in, w_ffn1_out, norm_mix, w_in, g_cq, w_uq, g_ckv, w_ukv,
                g_qn, g_kn, w_o, norm_ffn2, w_ffn2_in, w_ffn2_out):
    D = w_in.shape[1]

    def ffn_w(norm, wi, wo_):
        wa, wb = jnp.split(wi[l].astype(BF16), 2, axis=-1)
        return norm[l][None, :], wa, wb, wo_[l].astype(BF16)

    wi = w_in[l].astype(BF16)
    edges = [0]
    for w in (Q_LORA, KV_LORA, MLA_ROPE, GQA_HEADS * GQA_HD, GQA_KV_HEADS * GQA_HD,
              GQA_KV_HEADS * GQA_HD, D, D):
        edges.append(edges[-1] + w)
    c_q, c_kv, k_r, q_b, k_b, v_b, gate_a, gate_b = [wi[:, edges[i]:edges[i + 1]] for i in range(8)]
    half = MLA_ROPE // 2
    k_r_rot = jnp.concatenate([-k_r[:, half:], k_r[:, :half]], axis=-1)
    w1 = jnp.concatenate([c_q, c_kv, k_r, k_r_rot, k_b, gate_a, gate_b], axis=-1)
    w2t = jnp.concatenate([q_b, v_b], axis=-1).T
    wukv = w_ukv[l].astype(BF16).reshape(KV_LORA, MLA_HEADS, MLA_NOPE + MLA_V)
    wukvk = wukv[:, :, :MLA_NOPE].reshape(KV_LORA, MLA_HEADS * MLA_NOPE)
    wukvvt = wukv[:, :, MLA_NOPE:].reshape(KV_LORA, MLA_HEADS * MLA_V).T
    proj_w = (norm_mix[l][None, :], w1, w2t, g_cq[l][None, :], w_uq[l].astype(BF16).T,
              g_ckv[l][None, :], wukvk, wukvvt, g_qn[l][:, None], g_kn[l][None, :])
    return (ffn_w(norm_ffn1, w_ffn1_in, w_ffn1_out), proj_w, w_o[l].astype(BF16),
            ffn_w(norm_ffn2, w_ffn2_in, w_ffn2_out))


def _trunk(x, layers, norm_final):
    B, S, D = x.shape
    tables = _rope_tables(S)
    x2d = x.reshape(B * S, D)
    for i, (ffn1_w, proj_w, w_o, ffn2_w) in enumerate(layers):
        x2d = _ffn(x2d, *ffn1_w)
        qa, ka, va, qb, kb, vb, gates, knorm = _proj(x2d.reshape(B, S, D), proj_w, tables)
        merged = _attn(qa, ka, va, qb, kb, vb, gates, knorm)
        final_gain = norm_final[None, :] if i == len(layers) - 1 else None
        x2d = _ffn(x2d, *ffn2_w, merged=merged.reshape(B * S, D), w_o=w_o, final_gain=final_gain)
    return x2d.reshape(B, S, D)


def kernel(x_prompt, x_sample, norm_ffn1, w_ffn1_in, w_ffn1_out, norm_mix, w_in, g_cq, w_uq, g_ckv, w_ukv, g_qn, g_kn, w_o, norm_ffn2, w_ffn2_in, w_ffn2_out, norm_final):
    depth = w_in.shape[0]
    layers = [_prep_layer(l, norm_ffn1, w_ffn1_in, w_ffn1_out, norm_mix, w_in, g_cq, w_uq, g_ckv,
                          w_ukv, g_qn, g_kn, w_o, norm_ffn2, w_ffn2_in, w_ffn2_out)
              for l in range(depth)]
    return (_trunk(x_prompt, layers, norm_final), _trunk(x_sample, layers, norm_final))
```
